```python
import jax, jax.numpy as jnp
from jax import lax
import numpy as np

D_MODEL = 1024
BATCH = 16
SEQ = 4096
DEPTH = 1
DEC_BATCH = 128
DEC_SEQ = 1
PAST_LEN = 8192
PAGE_SIZE = 128

HEAD_DIM = 64
HEADS_PER_GROUP = 4
DIL_GROUPS = ((128, 1), (512, 4), (2048, 16))
N_HEADS_A = HEADS_PER_GROUP * len(DIL_GROUPS)
D_ATTN = N_HEADS_A * HEAD_DIM
D_GMLP = 512
GMLP_GROUPS = 4
GMLP_GROUP_DIM = D_GMLP // GMLP_GROUPS
CHUNK = 128
D_FF = 2816
N_ADA = 9
EPS = 1e-6
IN_SPLITS = (D_ATTN, 2 * D_ATTN, 3 * D_ATTN, 3 * D_ATTN + D_GMLP, 3 * D_ATTN + 2 * D_GMLP,
             3 * D_ATTN + 2 * D_GMLP + D_MODEL)
D_IN = 3 * D_ATTN + 2 * D_GMLP + 2 * D_MODEL

kernel_name = "hybrid_dilated_attn_gmlp_macaron_decode"


def rms_norm(x, g):
    xf = x.astype(jnp.float32)
    r = lax.rsqrt(jnp.mean(xf * xf, axis=-1, keepdims=True) + EPS)
    return (xf * r).astype(x.dtype) * g


def layer_norm(x, g, b):
    xf = x.astype(jnp.float32)
    mu = jnp.mean(xf, axis=-1, keepdims=True)
    var = jnp.mean(jnp.square(xf - mu), axis=-1, keepdims=True)
    return ((xf - mu) * lax.rsqrt(var + EPS)).astype(x.dtype) * g + b


def swiglu(h, w_up, w_down):
    a, b = jnp.split(h @ w_up, 2, axis=-1)
    return (jax.nn.silu(a) * b) @ w_down


def dilated_attn_prompt(q, k, v, window, dil):
    B, S, H, dh = q.shape
    nk = window // dil
    span = nk * dil
    s_pad = -(-S // span) * span
    nb = s_pad // span

    def to_blocks(t):
        t = jnp.pad(t, ((0, 0), (0, s_pad - S), (0, 0), (0, 0)))
        return t.reshape(B, nb, nk, dil, H, dh).transpose(0, 3, 4, 1, 2, 5)

    def with_prev(t):
        prev = jnp.pad(t, ((0, 0), (0, 0), (0, 0), (1, 0), (0, 0), (0, 0)))[:, :, :, :-1]
        return jnp.concatenate([prev, t], axis=4)

    qb = to_blocks(q)
    k2 = with_prev(to_blocks(k))
    v2 = with_prev(to_blocks(v))
    s = jnp.einsum("brhnid,brhnjd->brhnij", qb, k2,
                   preferred_element_type=jnp.float32) * (dh ** -0.5)
    i = jnp.arange(nk)[:, None]
    j = jnp.arange(2 * nk)[None, :]
    band = (j >= i) & (j <= i + nk)
    not_before_start = (jnp.arange(nb) > 0)[:, None, None] | (j >= nk)[None]
    mask = band[None] & not_before_start
    s = jnp.where(mask, s, -jnp.inf)
    lse = jax.nn.logsumexp(s, axis=-1)
    p = jnp.exp(s - lse[..., None]).astype(v.dtype)
    o = jnp.einsum("brhnij,brhnjd->brhnid", p, v2)
    o = o.transpose(0, 3, 4, 1, 2, 5).reshape(B, s_pad, H, dh)[:, :S]
    lse = lse.transpose(0, 3, 4, 1, 2).reshape(B, s_pad, H)[:, :S]
    return o, lse


def dilated_attn_sample(q, k_all, v_all, window, dil):
    Bd, T, H, dh = q.shape
    L = k_all.shape[1] - T
    nk = window // dil
    idx = (L + jnp.arange(T))[:, None] - dil * jnp.arange(nk + 1)[None, :]
    valid = idx >= 0
    idx = jnp.maximum(idx, 0)
    kg = k_all[:, idx]
    vg = v_all[:, idx]
    s = jnp.einsum("bthd,btkhd->bthk", q, kg,
                   preferred_element_type=jnp.float32) * (dh ** -0.5)
    s = jnp.where(valid[None, :, None, :], s, -jnp.inf)
    lse = jax.nn.logsumexp(s, axis=-1)
    p = jnp.exp(s - lse[..., None]).astype(v_all.dtype)
    o = jnp.einsum("bthk,btkhd->bthd", p, vg)
    return o, lse


def token_mix(h, caches, w_in, w_ba, w_bb, w_o, v_ln_g, v_ln_b, w_s, b_s):
    B, T, _ = h.shape
    proj = h @ w_in
    q, k, v, u, vb, ga, gb = jnp.split(proj, IN_SPLITS, axis=-1)
    shp = (B, T, N_HEADS_A, HEAD_DIM)
    q, k, v = q.reshape(shp), k.reshape(shp), v.reshape(shp)

    outs, lses, kv_states = [], [], []
    for g, (win, dil) in enumerate(DIL_GROUPS):
        sl = slice(g * HEADS_PER_GROUP, (g + 1) * HEADS_PER_GROUP)
        kv_new = jnp.stack([k[:, :, sl], v[:, :, sl]], axis=2)
        if caches is None:
            kv_all = kv_new
            o, lse = dilated_attn_prompt(q[:, :, sl], k[:, :, sl], v[:, :, sl], win, dil)
        else:
            kv_all = jnp.concatenate([caches[g], kv_new], axis=1)
            o, lse = dilated_attn_sample(q[:, :, sl], kv_all[:, :, 0], kv_all[:, :, 1], win, dil)
        kv_states.append(kv_all[:, kv_all.shape[1] - min(win, kv_all.shape[1]):])
        outs.append(o)
        lses.append(lse)
    wts = jax.nn.softmax(jnp.stack(lses, axis=0), axis=0)
    attn = (jnp.stack(outs, axis=0) * wts[..., None].astype(h.dtype))
    attn = attn.transpose(1, 2, 0, 3, 4).reshape(B, T, D_ATTN)

    vn = layer_norm(vb, v_ln_g, v_ln_b)
    ws = w_s * jnp.tril(jnp.ones((CHUNK, CHUNK), w_s.dtype))
    if caches is None:
        vc = vn.reshape(B, T // CHUNK, CHUNK, GMLP_GROUPS, GMLP_GROUP_DIM)
        mixed = jnp.einsum("gij,bcjgd->bcigd", ws, vc) + b_s.T[None, None, :, :, None]
        v_state = vn[:, T - CHUNK:]
    else:
        vc = vn.reshape(B, T, GMLP_GROUPS, GMLP_GROUP_DIM)
        mixed = jnp.einsum("gij,bjgd->bigd", ws[:, :T, :T], vc) + b_s[:, :T].T[None, :, :, None]
        v_state = vn
    gm = u * mixed.reshape(B, T, D_GMLP)

    merged = jax.nn.sigmoid(ga) * (attn @ w_ba) + jax.nn.sigmoid(gb) * (gm @ w_bb)
    return merged @ w_o, (kv_states[0], kv_states[1], kv_states[2], v_state)


def decoder_layer(x, c, caches, ada_w, ada_b, norm_g, ffn1_up, ffn1_down, w_in, w_ba, w_bb, w_o,
                  v_ln_g, v_ln_b, w_s, b_s, ffn2_up, ffn2_down):
    B = c.shape[0]
    mod = (jax.nn.silu(c) @ ada_w + ada_b).reshape(B, N_ADA, D_MODEL)[:, None]

    def pre(i, t):
        return rms_norm(t, norm_g[i]) * (1 + mod[:, :, 3 * i + 1]) + mod[:, :, 3 * i]

    x = x + 0.5 * mod[:, :, 2] * swiglu(pre(0, x), ffn1_up, ffn1_down)
    y, states = token_mix(pre(1, x), caches, w_in, w_ba, w_bb, w_o, v_ln_g, v_ln_b, w_s, b_s)
    x = x + mod[:, :, 5] * y
    x = x + 0.5 * mod[:, :, 8] * swiglu(pre(2, x), ffn2_up, ffn2_down)
    return x, states


def setup_inputs(seed: int = 0) -> dict:
    key = jax.random.key(seed)
    ks = jax.random.split(key, 24)
    f32 = jnp.float32

    def nrm(k, shape, scale):
        return jax.random.normal(k, shape, f32) * scale

    lens = [min(w, PAST_LEN) for w, _ in DIL_GROUPS]
    return {
        "x_prompt": nrm(ks[0], (BATCH, SEQ, D_MODEL), 1.0),
        "x_sample": nrm(ks[1], (DEC_BATCH, DEC_SEQ, D_MODEL), 1.0),
        "c_prompt": nrm(ks[2], (BATCH, D_MODEL), 1.0),
        "c_sample": nrm(ks[3], (DEC_BATCH, D_MODEL), 1.0),
        "cache_kv_g0": nrm(ks[4], (DEPTH, DEC_BATCH, lens[0], 2, HEADS_PER_GROUP, HEAD_DIM), 1.0),
        "cache_kv_g1": nrm(ks[5], (DEPTH, DEC_BATCH, lens[1], 2, HEADS_PER_GROUP, HEAD_DIM), 1.0),
        "cache_kv_g2": nrm(ks[6], (DEPTH, DEC_BATCH, lens[2], 2, HEADS_PER_GROUP, HEAD_DIM), 1.0),
        "ada_w": nrm(ks[7], (DEPTH, D_MODEL, N_ADA * D_MODEL), 0.5 * D_MODEL ** -0.5),
        "ada_b": nrm(ks[8], (DEPTH, N_ADA * D_MODEL), 0.02),
        "norm_g": 1.0 + nrm(ks[9], (DEPTH, 3, D_MODEL), 0.02),
        "ffn1_up": nrm(ks[10], (DEPTH, D_MODEL, 2 * D_FF), D_MODEL ** -0.5),
        "ffn1_down": nrm(ks[11], (DEPTH, D_FF, D_MODEL), D_FF ** -0.5),
        "w_in": nrm(ks[12], (DEPTH, D_MODEL, D_IN), D_MODEL ** -0.5),
        "w_branch_a": nrm(ks[13], (DEPTH, D_ATTN, D_MODEL), D_ATTN ** -0.5),
        "w_branch_b": nrm(ks[14], (DEPTH, D_GMLP, D_MODEL), D_GMLP ** -0.5),
        "w_out": nrm(ks[15], (DEPTH, D_MODEL, D_MODEL), D_MODEL ** -0.5),
        "v_ln_g": 1.0 + nrm(ks[16], (DEPTH, D_GMLP), 0.02),
        "v_ln_b": nrm(ks[17], (DEPTH, D_GMLP), 0.02),
        "w_spatial": nrm(ks[18], (DEPTH, GMLP_GROUPS, CHUNK, CHUNK), CHUNK ** -0.5),
        "b_spatial": 1.0 + nrm(ks[19], (DEPTH, GMLP_GROUPS, CHUNK), 0.02),
        "ffn2_up": nrm(ks[20], (DEPTH, D_MODEL, 2 * D_FF), D_MODEL ** -0.5),
        "ffn2_down": nrm(ks[21], (DEPTH, D_FF, D_MODEL), D_FF ** -0.5),
        "final_g": 1.0 + nrm(ks[22], (D_MODEL,), 0.02),
    }


def reference(x_prompt, x_sample, c_prompt, c_sample, cache_kv_g0, cache_kv_g1, cache_kv_g2,
              ada_w, ada_b, norm_g, ffn1_up, ffn1_down, w_in, w_branch_a, w_branch_b, w_out,
              v_ln_g, v_ln_b, w_spatial, b_spatial, ffn2_up, ffn2_down, final_g):
    hp, hs = x_prompt, x_sample
    st_p, st_s = [], []
    for l in range(DEPTH):
        weights = (ada_w[l], ada_b[l], norm_g[l], ffn1_up[l], ffn1_down[l], w_in[l], w_branch_a[l],
                   w_branch_b[l], w_out[l], v_ln_g[l], v_ln_b[l], w_spatial[l], b_spatial[l],
                   ffn2_up[l], ffn2_down[l])
        hp, sp = decoder_layer(hp, c_prompt, None, *weights)
        hs, ss = decoder_layer(hs, c_sample, (cache_kv_g0[l], cache_kv_g1[l], cache_kv_g2[l]), *weights)
        st_p.append(sp)
        st_s.append(ss)
    y_prompt = rms_norm(hp, final_g)
    y_sample = rms_norm(hs, final_g)
    kv_g0_prompt = jnp.stack([s[0] for s in st_p])
    kv_g1_prompt = jnp.stack([s[1] for s in st_p])
    kv_g2_prompt = jnp.stack([s[2] for s in st_p])
    vrows_prompt = jnp.stack([s[3] for s in st_p])
    kv_g0_sample = jnp.stack([s[0] for s in st_s])
    kv_g1_sample = jnp.stack([s[1] for s in st_s])
    kv_g2_sample = jnp.stack([s[2] for s in st_s])
    vrows_sample = jnp.stack([s[3] for s in st_s])
    return (y_prompt, y_sample, kv_g0_prompt, kv_g1_prompt, kv_g2_prompt, vrows_prompt,
            kv_g0_sample, kv_g1_sample, kv_g2_sample, vrows_sample)
```

```python
import functools

import jax
import jax.numpy as jnp
from jax import lax
from jax.experimental import pallas as pl
from jax.experimental.pallas import tpu as pltpu

F32 = jnp.float32
BF16 = jnp.bfloat16

D_MODEL = 1024
HEAD_DIM = 64
HEADS_PER_GROUP = 4
GROUP_W = HEADS_PER_GROUP * HEAD_DIM
DIL_GROUPS = ((128, 1), (512, 4), (2048, 16))
N_GROUPS = len(DIL_GROUPS)
D_ATTN = N_GROUPS * GROUP_W
D_GMLP = 512
GMLP_GROUPS = 4
CHUNK = 128
D_FF = 2816
FF_CHUNK = 256
N_FF_CHUNKS = D_FF // FF_CHUNK
EPS = 1e-6
KEYS_PER_QUERY_BLOCK = 128
MASKED = -1e30
LANES = 128

VMEM_LIMIT_BYTES = 60 * 1024 * 1024
ROW_TILE = 512
WINDOW_BLOCKS_PER_STEP = 8


def _resident(shape):
    zeros = (0,) * len(shape)
    return pl.BlockSpec(shape, lambda *_: zeros, pipeline_mode=pl.Buffered(1))


def _whole(shape):
    zeros = (0,) * len(shape)
    return pl.BlockSpec(shape, lambda *_: zeros)


def _rms(x):
    return x * lax.rsqrt(jnp.mean(x * x, axis=-1, keepdims=True) + EPS)


def _mod(mod_ref, i):
    if len(mod_ref.shape) == 2:
        return mod_ref[i:i + 1, :]
    return mod_ref[i]


def _pre(x, g, mod_ref, sub):
    return (_rms(x) * g) * (1.0 + _mod(mod_ref, 3 * sub + 1)) + _mod(mod_ref, 3 * sub)


def _swiglu(h_ref, wa_ref, wb_ref, wd_ref, acc_ref):
    acc_ref[...] = jnp.zeros_like(acc_ref)

    def body(c, carry):
        h = h_ref[...]
        a = jnp.dot(h, wa_ref[c], preferred_element_type=F32)
        b = jnp.dot(h, wb_ref[c], preferred_element_type=F32)
        act = ((a * jax.nn.sigmoid(a)) * b).astype(BF16)
        acc_ref[...] += jnp.dot(act, wd_ref[c], preferred_element_type=F32)
        return carry

    lax.fori_loop(0, N_FF_CHUNKS, body, 0)
    return acc_ref[...]


def _ada_kernel(c_ref, w_ref, b_ref, o_ref):
    c = c_ref[...]
    h = (c * jax.nn.sigmoid(c)).astype(BF16)
    o_ref[...] = jnp.dot(h, w_ref[...].astype(BF16), preferred_element_type=F32) + b_ref[...]


def _ada_table(c_all, ada_w, ada_b):
    n = c_all.shape[0]
    n_ada = ada_w.shape[1] // D_MODEL
    return pl.pallas_call(
        _ada_kernel,
        grid=(n_ada,),
        in_specs=[
            pl.BlockSpec((n, D_MODEL), lambda i: (0, 0)),
            pl.BlockSpec((D_MODEL, D_MODEL), lambda i: (0, i)),
            pl.BlockSpec((1, D_MODEL), lambda i: (0, i)),
        ],
        out_specs=pl.BlockSpec((None, n, D_MODEL), lambda i: (i, 0, 0)),
        out_shape=jax.ShapeDtypeStruct((n_ada, n, D_MODEL), F32),
        compiler_params=pltpu.CompilerParams(dimension_semantics=("arbitrary",)),
        name="k_ada",
    )(c_all, ada_w, ada_b)


def _ffn1_front(x_ref, mod_ref, ng_ref, wa_ref, wb_ref, wd_ref, hb_scr, acc_scr):
    x = x_ref[...]
    hb_scr[...] = _pre(x, ng_ref[0:1, :], mod_ref, 0).astype(BF16)
    f = _swiglu(hb_scr, wa_ref, wb_ref, wd_ref, acc_scr)
    x1 = x + (0.5 * _mod(mod_ref, 2)) * f
    return x1, _pre(x1, ng_ref[1:2, :], mod_ref, 1)


def _ffn_qkv_prompt_kernel(x_ref, mod_ref, ng_ref, wa_ref, wb_ref, wd_ref, wqkv_ref, wkvt_ref,
                           x1_ref, a0_ref, a1_ref, a2_ref, kv0_ref, kv1_ref, kv2_ref,
                           hb_scr, acc_scr, h1_scr, hp_scr, *, seq_len):
    tm = x_ref.shape[0]
    i = pl.program_id(1)
    x1, h1 = _ffn1_front(x_ref, mod_ref, ng_ref, wa_ref, wb_ref, wd_ref, hb_scr, acc_scr)
    x1_ref[...] = x1
    for c in range(D_MODEL // LANES):
        h1_scr[c] = h1[:, c * LANES:(c + 1) * LANES]
    hb_scr[...] = h1.astype(BF16)

    for g, a_ref in enumerate((a0_ref, a1_ref, a2_ref)):
        dil = DIL_GROUPS[g][1]
        if dil == 1:
            hp = hb_scr[...]
        else:
            n = tm // dil
            for r in range(dil):
                rows = [h1_scr[c, pl.ds(r, n, stride=dil), :] for c in range(D_MODEL // LANES)]
                hp_scr[r * n:(r + 1) * n, :] = jnp.concatenate(rows, axis=1).astype(BF16)
            hp = hp_scr[...]
        qkv = jnp.dot(hp, wqkv_ref[g], preferred_element_type=F32)
        a_ref[...] = qkv.astype(BF16).reshape(a_ref.shape)

    for g, kv_ref in enumerate((kv0_ref, kv1_ref, kv2_ref)):
        window = DIL_GROUPS[g][0]
        first_tile = (seq_len - window) // tm

        @pl.when(i >= first_tile)
        def _(g=g, kv_ref=kv_ref, window=window):
            kvt = lax.dot_general(wkvt_ref[g], hb_scr[...], (((1,), (1,)), ((), ())),
                                  preferred_element_type=F32)
            if window < tm:
                kvt = kvt[:, tm - window:]
            kv_ref[...] = kvt.reshape(kv_ref.shape)


def _ffn_qkv_prompt(x, mod, norm_g, wa, wb, wd, wqkv, wkvt):
    bsz, seq_len, _ = x.shape
    tm = ROW_TILE
    weights = (norm_g, wa, wb, wd, wqkv, wkvt)
    a_shapes, a_specs, kv_shapes, kv_specs = [], [], [], []
    for window, dil in DIL_GROUPS:
        a_shapes.append(jax.ShapeDtypeStruct((bsz, dil, seq_len // dil, 3 * GROUP_W), BF16))
        a_specs.append(pl.BlockSpec((None, dil, tm // dil, 3 * GROUP_W), lambda b, i: (b, 0, i, 0)))
        first_tile = (seq_len - window) // tm
        kv_shapes.append(jax.ShapeDtypeStruct((bsz, 2 * HEADS_PER_GROUP, HEAD_DIM, window), F32))
        kv_specs.append(pl.BlockSpec(
            (None, 2 * HEADS_PER_GROUP, HEAD_DIM, min(window, tm)),
            lambda b, i, first_tile=first_tile: (b, 0, 0, jnp.maximum(i - first_tile, 0))))
    row_spec = pl.BlockSpec((None, tm, D_MODEL), lambda b, i: (b, i, 0))
    return pl.pallas_call(
        functools.partial(_ffn_qkv_prompt_kernel, seq_len=seq_len),
        grid=(bsz, seq_len // tm),
        in_specs=[row_spec, pl.BlockSpec((None,) + mod.shape[1:], lambda b, i: (b, 0, 0))]
        + [_resident(w.shape) for w in weights],
        out_specs=[row_spec] + a_specs + kv_specs,
        out_shape=[jax.ShapeDtypeStruct(x.shape, F32)] + a_shapes + kv_shapes,
        scratch_shapes=[pltpu.VMEM((tm, D_MODEL), BF16), pltpu.VMEM((tm, D_MODEL), F32),
                        pltpu.VMEM((D_MODEL // LANES, tm, LANES), F32), pltpu.VMEM((tm, D_MODEL), BF16)],
        compiler_params=pltpu.CompilerParams(dimension_semantics=("arbitrary", "arbitrary"),
                                             vmem_limit_bytes=VMEM_LIMIT_BYTES),
        name="k_ffn_qkv",
    )(x, mod, *weights)


def _ffn_qkv_sample_kernel(x_ref, mod_ref, ng_ref, wa_ref, wb_ref, wd_ref, wqkvt_ref,
                           x1_ref, qkvt_ref, hb_scr, acc_scr):
    x1, h1 = _ffn1_front(x_ref, mod_ref, ng_ref, wa_ref, wb_ref, wd_ref, hb_scr, acc_scr)
    x1_ref[...] = x1
    qkvt_ref[...] = lax.dot_general(wqkvt_ref[...], h1.astype(BF16), (((1,), (1,)), ((), ())),
                                    preferred_element_type=F32)


def _ffn_qkv_sample(x, mod, norm_g, wa, wb, wd, wqkvt):
    n = x.shape[0]
    weights = (norm_g, wa, wb, wd, wqkvt)
    return pl.pallas_call(
        _ffn_qkv_sample_kernel,
        grid=(1,),
        in_specs=[_resident(x.shape), _resident(mod.shape)] + [_resident(w.shape) for w in weights],
        out_specs=[_whole(x.shape), _whole((3 * D_ATTN, n))],
        out_shape=[jax.ShapeDtypeStruct(x.shape, F32), jax.ShapeDtypeStruct((3 * D_ATTN, n), F32)],
        scratch_shapes=[pltpu.VMEM((n, D_MODEL), BF16), pltpu.VMEM((n, D_MODEL), F32)],
        compiler_params=pltpu.CompilerParams(dimension_semantics=("arbitrary",),
                                             vmem_limit_bytes=VMEM_LIMIT_BYTES),
        name="k_ffn_qkv_sample",
    )(x, mod, *weights)


def _window_kernel(cur_ref, prev_ref, og_ref, kv_scr, *, blocks_per_seq):
    tbb = cur_ref.shape[0]
    nq = KEYS_PER_QUERY_BLOCK
    j = pl.program_id(0)
    kv_scr[0:nq, :] = prev_ref[0, :, GROUP_W:]
    for t in range(tbb):
        kv_scr[(t + 1) * nq:(t + 2) * nq, :] = cur_ref[t, :, GROUP_W:]

    lane_head = lax.broadcasted_iota(jnp.int32, (nq, GROUP_W), 1) // HEAD_DIM
    qi = lax.broadcasted_iota(jnp.int32, (HEADS_PER_GROUP * nq, 2 * nq), 0) % nq
    kj = lax.broadcasted_iota(jnp.int32, (HEADS_PER_GROUP * nq, 2 * nq), 1)
    band = (kj >= qi) & (kj <= qi + nq)

    def body(t, carry):
        q = cur_ref[t, :, 0:GROUP_W]
        start = pl.multiple_of(t * nq, nq)
        k2 = kv_scr[pl.ds(start, 2 * nq), 0:GROUP_W]
        v2 = kv_scr[pl.ds(start, 2 * nq), GROUP_W:]
        qs = jnp.concatenate([jnp.where(lane_head == h, q, jnp.zeros_like(q))
                              for h in range(HEADS_PER_GROUP)], axis=0)
        s = lax.dot_general(qs, k2, (((1,), (1,)), ((), ())), preferred_element_type=F32)
        first_key = jnp.where(((j * tbb + t) % blocks_per_seq) == 0, nq, 0)
        valid = band & (kj >= first_key)
        s = jnp.where(valid, s, MASKED)
        m = jnp.max(s, axis=1, keepdims=True)
        p = jnp.exp(s - m)
        l = jnp.sum(p, axis=1, keepdims=True)
        r = jnp.dot(p.astype(BF16), v2, preferred_element_type=F32)
        inv_l = 1.0 / l
        lse = m + jnp.log(l)
        o = jnp.zeros((nq, GROUP_W), F32)
        lse_b = jnp.zeros((nq, GROUP_W), F32)
        for h in range(HEADS_PER_GROUP):
            rows = slice(h * nq, (h + 1) * nq)
            o = jnp.where(lane_head == h, r[rows] * inv_l[rows], o)
            lse_b = jnp.where(lane_head == h, lse[rows], lse_b)
        og_ref[t, :, 0:GROUP_W] = o
        og_ref[t, :, GROUP_W:] = lse_b
        return carry

    lax.fori_loop(0, tbb, body, 0)


def _window_attention(a_blocks, blocks_per_seq):
    n_blocks = a_blocks.shape[0]
    tbb = WINDOW_BLOCKS_PER_STEP
    nq = KEYS_PER_QUERY_BLOCK
    return pl.pallas_call(
        functools.partial(_window_kernel, blocks_per_seq=blocks_per_seq),
        grid=(n_blocks // tbb,),
        in_specs=[pl.BlockSpec((tbb, nq, 3 * GROUP_W), lambda j: (j, 0, 0)),
                  pl.BlockSpec((1, nq, 3 * GROUP_W), lambda j: (jnp.maximum(j * tbb - 1, 0), 0, 0))],
        out_specs=pl.BlockSpec((tbb, nq, 2 * GROUP_W), lambda j: (j, 0, 0)),
        out_shape=jax.ShapeDtypeStruct((n_blocks, nq, 2 * GROUP_W), F32),
        scratch_shapes=[pltpu.VMEM(((tbb + 1) * nq, 2 * GROUP_W), BF16)],
        compiler_params=pltpu.CompilerParams(dimension_semantics=("arbitrary",)),
        name="k_window",
    )(a_blocks, a_blocks)


def _merge_groups(o_and_lse):
    m = o_and_lse[0][1]
    for _, lse in o_and_lse[1:]:
        m = jnp.maximum(m, lse)
    es = [jnp.exp(lse - m) for _, lse in o_and_lse]
    inv = 1.0 / (es[0] + es[1] + es[2])
    return jnp.concatenate([(o * (e * inv)).astype(BF16) for (o, _), e in zip(o_and_lse, es)], axis=1)


def _mix_tail(x1, h1, u, mixed, attn, mod_ref, ng_ref, wgg_ref, wba_ref, wbb_ref, wo_ref,
              wa_ref, wb_ref, wd_ref, fg_ref, hb_scr, acc_scr):
    gm = (u * mixed).astype(BF16)
    gg = jnp.dot(h1, wgg_ref[...], preferred_element_type=F32)
    merged = (jax.nn.sigmoid(gg[:, :D_MODEL]) * jnp.dot(attn, wba_ref[...], preferred_element_type=F32)
              + jax.nn.sigmoid(gg[:, D_MODEL:]) * jnp.dot(gm, wbb_ref[...], preferred_element_type=F32))
    y = jnp.dot(merged.astype(BF16), wo_ref[...], preferred_element_type=F32)
    x2 = x1 + _mod(mod_ref, 5) * y
    hb_scr[...] = _pre(x2, ng_ref[2:3, :], mod_ref, 2).astype(BF16)
    f = _swiglu(hb_scr, wa_ref, wb_ref, wd_ref, acc_scr)
    x3 = x2 + (0.5 * _mod(mod_ref, 8)) * f
    return _rms(x3) * fg_ref[...]


def _layer_norm_rows(v, g_ref, b_ref):
    mu = jnp.mean(v, axis=-1, keepdims=True)
    var = jnp.mean(jnp.square(v - mu), axis=-1, keepdims=True)
    return ((v - mu) * lax.rsqrt(var + EPS)) * g_ref[...] + b_ref[...]


def _mix_ffn_prompt_kernel(x1_ref, og0_ref, og1_ref, og2_ref, mod_ref, ng_ref, wuv_ref, wgg_ref, lng_ref, lnb_ref,
                           ws_ref, bst_ref, wba_ref, wbb_ref, wo_ref, wa_ref, wb_ref, wd_ref, fg_ref,
                           y_ref, vrows_ref, hb_scr, acc_scr, un_scr, mixed_scr):
    tm = x1_ref.shape[0]
    i = pl.program_id(1)
    x1 = x1_ref[...]
    h1 = _pre(x1, ng_ref[1:2, :], mod_ref, 1).astype(BF16)
    uv = jnp.dot(h1, wuv_ref[...], preferred_element_type=F32)
    u = uv[:, :D_GMLP]
    vn = _layer_norm_rows(uv[:, D_GMLP:], lng_ref, lnb_ref)

    @pl.when(i == pl.num_programs(1) - 1)
    def _():
        vrows_ref[...] = vn[tm - CHUNK:, :]

    vnb = vn.astype(BF16)
    tri = (lax.broadcasted_iota(jnp.int32, (CHUNK, CHUNK), 0)
           >= lax.broadcasted_iota(jnp.int32, (CHUNK, CHUNK), 1))
    gw = D_GMLP // GMLP_GROUPS
    for g in range(GMLP_GROUPS):
        ws = jnp.where(tri, ws_ref[g], 0.0).astype(BF16)
        bias = bst_ref[:, g:g + 1]
        for c in range(tm // CHUNK):
            mixed_scr[c * CHUNK:(c + 1) * CHUNK, g * gw:(g + 1) * gw] = (
                jnp.dot(ws, vnb[c * CHUNK:(c + 1) * CHUNK, g * gw:(g + 1) * gw],
                        preferred_element_type=F32) + bias)

    for g, og_ref in enumerate((og0_ref, og1_ref, og2_ref)):
        dil = DIL_GROUPS[g][1]
        n = tm // dil
        for r in range(dil):
            for c in range(2 * GROUP_W // LANES):
                un_scr[g, c, pl.ds(r, n, stride=dil), :] = og_ref[r, :, c * LANES:(c + 1) * LANES]
    o_and_lse = []
    half = GROUP_W // LANES
    for g in range(N_GROUPS):
        cols = [un_scr[g, c] for c in range(2 * half)]
        o_and_lse.append((jnp.concatenate(cols[:half], axis=1), jnp.concatenate(cols[half:], axis=1)))
    attn = _merge_groups(o_and_lse)

    y_ref[...] = _mix_tail(x1, h1, u, mixed_scr[...], attn, mod_ref, ng_ref, wgg_ref, wba_ref, wbb_ref, wo_ref,
                           wa_ref, wb_ref, wd_ref, fg_ref, hb_scr, acc_scr)


def _mix_ffn_prompt(x1, ogs, mod, norm_g, wuv, wgg, lng, lnb, ws, bst, wba, wbb, wo, wa, wb, wd, fg):
    bsz, seq_len, _ = x1.shape
    tm = ROW_TILE
    weights = (norm_g, wuv, wgg, lng, lnb, ws, bst, wba, wbb, wo, wa, wb, wd, fg)
    row_spec = pl.BlockSpec((None, tm, D_MODEL), lambda b, i: (b, i, 0))
    og_specs = [pl.BlockSpec((None, dil, tm // dil, 2 * GROUP_W), lambda b, i: (b, 0, i, 0))
                for _, dil in DIL_GROUPS]
    return pl.pallas_call(
        _mix_ffn_prompt_kernel,
        grid=(bsz, seq_len // tm),
        in_specs=[row_spec] + og_specs + [pl.BlockSpec((None,) + mod.shape[1:], lambda b, i: (b, 0, 0))]
        + [_resident(w.shape) for w in weights],
        out_specs=[row_spec, pl.BlockSpec((None, CHUNK, D_GMLP), lambda b, i: (b, 0, 0))],
        out_shape=[jax.ShapeDtypeStruct(x1.shape, F32), jax.ShapeDtypeStruct((bsz, CHUNK, D_GMLP), F32)],
        scratch_shapes=[pltpu.VMEM((tm, D_MODEL), BF16), pltpu.VMEM((tm, D_MODEL), F32),
                        pltpu.VMEM((N_GROUPS, 2 * GROUP_W // LANES, tm, LANES), F32),
                        pltpu.VMEM((tm, D_GMLP), F32)],
        compiler_params=pltpu.CompilerParams(dimension_semantics=("arbitrary", "arbitrary"),
                                             vmem_limit_bytes=VMEM_LIMIT_BYTES),
        name="k_mix_ffn",
    )(x1, *ogs, mod, *weights)


def _mix_ffn_sample_kernel(x1_ref, attnt_ref, mod_ref, ng_ref, wuv_ref, wgg_ref, lng_ref, lnb_ref,
                           ws0_ref, bs0_ref, wba_ref, wbb_ref, wo_ref, wa_ref, wb_ref, wd_ref, fg_ref,
                           y_ref, vrows_ref, hb_scr, acc_scr):
    x1 = x1_ref[...]
    h1 = _pre(x1, ng_ref[1:2, :], mod_ref, 1).astype(BF16)
    uv = jnp.dot(h1, wuv_ref[...], preferred_element_type=F32)
    u = uv[:, :D_GMLP]
    vn = _layer_norm_rows(uv[:, D_GMLP:], lng_ref, lnb_ref)
    vrows_ref[...] = vn
    mixed = ws0_ref[...] * vn + bs0_ref[...]
    attn = attnt_ref[...].T.astype(BF16)
    y_ref[...] = _mix_tail(x1, h1, u, mixed, attn, mod_ref, ng_ref, wgg_ref, wba_ref, wbb_ref, wo_ref,
                           wa_ref, wb_ref, wd_ref, fg_ref, hb_scr, acc_scr)


def _mix_ffn_sample(x1, attnt, mod, norm_g, wuv, wgg, lng, lnb, ws0, bs0, wba, wbb, wo, wa, wb, wd, fg):
    n = x1.shape[0]
    operands = (x1, attnt, mod, norm_g, wuv, wgg, lng, lnb, ws0, bs0, wba, wbb, wo, wa, wb, wd, fg)
    return pl.pallas_call(
        _mix_ffn_sample_kernel,
        grid=(1,),
        in_specs=[_resident(w.shape) for w in operands],
        out_specs=[_whole(x1.shape), _whole((n, D_GMLP))],
        out_shape=[jax.ShapeDtypeStruct(x1.shape, F32), jax.ShapeDtypeStruct((n, D_GMLP), F32)],
        scratch_shapes=[pltpu.VMEM((n, D_MODEL), BF16), pltpu.VMEM((n, D_MODEL), F32)],
        compiler_params=pltpu.CompilerParams(dimension_semantics=("arbitrary",),
                                             vmem_limit_bytes=VMEM_LIMIT_BYTES),
        name="k_mix_ffn_sample",
    )(*operands)


def _sample_attn_kernel(qkvt_ref, c0_ref, c1_ref, c2_ref, o0_ref, o1_ref, o2_ref, attnt_ref):
    b = pl.program_id(0)
    n_req = qkvt_ref.shape[1]
    onehot = (lax.broadcasted_iota(jnp.int32, (1, n_req), 1) == b).astype(F32)
    cols = jnp.sum(qkvt_ref[...] * onehot, axis=1, keepdims=True)

    @pl.when(b == 0)
    def _():
        attnt_ref[...] = jnp.zeros_like(attnt_ref)

    outs = []
    for g, (c_ref, o_ref) in enumerate(((c0_ref, o0_ref), (c1_ref, o1_ref), (c2_ref, o2_ref))):
        window, dil = DIL_GROUPS[g]
        rows = c_ref.shape[-1]
        col = lax.broadcasted_iota(jnp.int32, (1, rows), 1)
        dist = rows - col
        strided = (dist % dil == 0) & (dist <= window)
        newest = col == rows - 1
        for h in range(HEADS_PER_GROUP):
            base = g * GROUP_W + h * HEAD_DIM
            qc = cols[base:base + HEAD_DIM]
            kc = cols[D_ATTN + base:D_ATTN + base + HEAD_DIM]
            vc = cols[2 * D_ATTN + base:2 * D_ATTN + base + HEAD_DIM]
            kt = c_ref[h]
            vt = c_ref[HEADS_PER_GROUP + h]
            s = jnp.sum(kt * qc, axis=0, keepdims=True)
            s = jnp.where(strided, s, MASKED)
            s_new = jnp.sum(kc * qc, axis=0, keepdims=True)
            m = jnp.maximum(jnp.max(s, axis=1, keepdims=True), s_new)
            p = jnp.exp(s - m)
            p_new = jnp.exp(s_new - m)
            l = jnp.sum(p, axis=1, keepdims=True) + p_new
            o = (jnp.sum(vt * p, axis=1, keepdims=True) + vc * p_new) / l
            outs.append((base, o, m + jnp.log(l)))
            o_ref[h] = jnp.where(newest, kc, pltpu.roll(kt, rows - 1, axis=1))
            o_ref[HEADS_PER_GROUP + h] = jnp.where(newest, vc, pltpu.roll(vt, rows - 1, axis=1))

    for h in range(HEADS_PER_GROUP):
        per_group = [outs[g * HEADS_PER_GROUP + h] for g in range(N_GROUPS)]
        m = jnp.maximum(jnp.maximum(per_group[0][2], per_group[1][2]), per_group[2][2])
        es = [jnp.exp(lse - m) for _, _, lse in per_group]
        inv = 1.0 / (es[0] + es[1] + es[2])
        for (base, o, _), e in zip(per_group, es):
            attnt_ref[base:base + HEAD_DIM, :] += (o * (e * inv)) * onehot


def _sample_attention(qkvt, caches):
    n_req = qkvt.shape[1]
    c_specs = [pl.BlockSpec((None,) + c.shape[1:], lambda b: (b, 0, 0, 0)) for c in caches]
    return pl.pallas_call(
        _sample_attn_kernel,
        grid=(n_req,),
        in_specs=[_resident(qkvt.shape)] + c_specs,
        out_specs=c_specs + [pl.BlockSpec((D_ATTN, n_req), lambda b: (0, 0))],
        out_shape=[jax.ShapeDtypeStruct(c.shape, F32) for c in caches]
        + [jax.ShapeDtypeStruct((D_ATTN, n_req), F32)],
        compiler_params=pltpu.CompilerParams(dimension_semantics=("arbitrary",),
                                             vmem_limit_bytes=VMEM_LIMIT_BYTES),
        name="k_sample_attn",
    )(qkvt, *caches)


def _to_buffer_layout(cache):
    n, rows = cache.shape[0], cache.shape[1]
    return cache.transpose(0, 2, 3, 4, 1).reshape(n, 2 * HEADS_PER_GROUP, HEAD_DIM, rows)


def _from_buffer_layout(buf):
    n, rows = buf.shape[0], buf.shape[-1]
    return buf.reshape(n, 2, HEADS_PER_GROUP, HEAD_DIM, rows).transpose(0, 4, 1, 2, 3)[None]


def _split_ffn(w_up, w_down):
    wa = w_up[:, :D_FF].reshape(D_MODEL, N_FF_CHUNKS, FF_CHUNK).transpose(1, 0, 2).astype(BF16)
    wb = w_up[:, D_FF:].reshape(D_MODEL, N_FF_CHUNKS, FF_CHUNK).transpose(1, 0, 2).astype(BF16)
    wd = w_down.reshape(N_FF_CHUNKS, FF_CHUNK, D_MODEL).astype(BF16)
    return wa, wb, wd


def kernel(x_prompt, x_sample, c_prompt, c_sample, cache_kv_g0, cache_kv_g1, cache_kv_g2, ada_w, ada_b, norm_g,
           ffn1_up, ffn1_down, w_in, w_branch_a, w_branch_b, w_out, v_ln_g, v_ln_b, w_spatial, b_spatial,
           ffn2_up, ffn2_down, final_g):
    assert ada_w.shape[0] == 1, "one layer"
    bsz, seq_len, _ = x_prompt.shape
    n_req = x_sample.shape[0]

    wa1, wb1, wd1 = _split_ffn(ffn1_up[0], ffn1_down[0])
    wa2, wb2, wd2 = _split_ffn(ffn2_up[0], ffn2_down[0])
    win = w_in[0]
    wq = win[:, :D_ATTN] * (HEAD_DIM ** -0.5)
    wk = win[:, D_ATTN:2 * D_ATTN]
    wv = win[:, 2 * D_ATTN:3 * D_ATTN]
    grp = lambda w, g: w[:, g * GROUP_W:(g + 1) * GROUP_W]
    wqkv = jnp.stack([jnp.concatenate([grp(wq, g), grp(wk, g), grp(wv, g)], axis=1)
                      for g in range(N_GROUPS)]).astype(BF16)
    wkvt = jnp.stack([jnp.concatenate([grp(wk, g), grp(wv, g)], axis=1).T for g in range(N_GROUPS)]).astype(BF16)
    wqkvt = jnp.concatenate([wq, wk, wv], axis=1).T.astype(BF16)
    wuv = win[:, 3 * D_ATTN:3 * D_ATTN + 2 * D_GMLP].astype(BF16)
    wgg = win[:, 3 * D_ATTN + 2 * D_GMLP:].astype(BF16)
    wba, wbb, wo = w_branch_a[0].astype(BF16), w_branch_b[0].astype(BF16), w_out[0].astype(BF16)
    ng = norm_g[0]
    lng, lnb = v_ln_g, v_ln_b
    ws = w_spatial[0]
    bst = b_spatial[0].T
    gw = D_GMLP // GMLP_GROUPS
    ws0 = jnp.repeat(ws[:, 0, 0], gw)[None, :]
    bs0 = jnp.repeat(b_spatial[0][:, 0], gw)[None, :]
    fg = final_g[None, :]

    mods = _ada_table(jnp.concatenate([c_prompt, c_sample], axis=0), ada_w[0], ada_b)
    mod_p = mods[:, :bsz].transpose(1, 0, 2)
    mod_s = mods[:, bsz:]

    x1, a0, a1, a2, kv0, kv1, kv2 = _ffn_qkv_prompt(x_prompt, mod_p, ng, wa1, wb1, wd1, wqkv, wkvt)
    ogs = []
    for a, (_, dil) in zip((a0, a1, a2), DIL_GROUPS):
        blocks = a.reshape(-1, KEYS_PER_QUERY_BLOCK, 3 * GROUP_W)
        og = _window_attention(blocks, seq_len // dil // KEYS_PER_QUERY_BLOCK)
        ogs.append(og.reshape(bsz, dil, seq_len // dil, 2 * GROUP_W))
    y_prompt, vrows_p = _mix_ffn_prompt(x1, ogs, mod_p, ng, wuv, wgg, lng, lnb, ws, bst, wba, wbb, wo,
                                        wa2, wb2, wd2, fg)

    xs = x_sample.reshape(n_req, D_MODEL)
    x1s, qkvt = _ffn_qkv_sample(xs, mod_s, ng, wa1, wb1, wd1, wqkvt)
    caches = [_to_buffer_layout(c[0]) for c in (cache_kv_g0, cache_kv_g1, cache_kv_g2)]
    s0, s1, s2, attnt = _sample_attention(qkvt, caches)
    y_sample, vrows_s = _mix_ffn_sample(x1s, attnt, mod_s, ng, wuv, wgg, lng, lnb, ws0, bs0, wba, wbb, wo,
                                        wa2, wb2, wd2, fg)

    return (y_prompt, y_sample.reshape(n_req, 1, D_MODEL),
            _from_buffer_layout(kv0), _from_buffer_layout(kv1), _from_buffer_layout(kv2), vrows_p[None],
            _from_buffer_layout(s0), _from_buffer_layout(s1), _from_buffer_layout(s2),
            vrows_s.reshape(1, n_req, 1, D_GMLP))
```

```python
import functools

import jax
import jax.numpy as jnp
from jax import lax
from jax.experimental import pallas as pl
from jax.experimental.pallas import tpu as pltpu

F32 = jnp.float32
BF16 = jnp.bfloat16

D_MODEL = 1024
HEAD_DIM = 64
HEADS_PER_GROUP = 4
GROUP_W = HEADS_PER_GROUP * HEAD_DIM
DIL_GROUPS = ((128, 1), (512, 4), (2048, 16))
N_GROUPS = len(DIL_GROUPS)
D_ATTN = N_GROUPS * GROUP_W
D_GMLP = 512
GMLP_GROUPS = 4
CHUNK = 128
D_FF = 2816
FF_CHUNK = 256
N_FF_CHUNKS = D_FF // FF_CHUNK
EPS = 1e-6
QK_SCALE = HEAD_DIM ** -0.5
KEYS_PER_QUERY_BLOCK = 128
MASKED = -1e30
LANES = 128

VMEM_LIMIT_BYTES = 60 * 1024 * 1024
ROW_TILE = 512
WINDOW_BLOCKS_PER_STEP = 8
WINDOW_UNROLL = 8


def _resident(shape):
    zeros = (0,) * len(shape)
    return pl.BlockSpec(shape, lambda *_: zeros, pipeline_mode=pl.Buffered(1))


def _whole(shape):
    zeros = (0,) * len(shape)
    return pl.BlockSpec(shape, lambda *_: zeros)


def _rms(x):
    return x * lax.rsqrt(jnp.mean(x * x, axis=-1, keepdims=True) + EPS)


def _mod(mod_ref, i):
    if len(mod_ref.shape) == 2:
        return mod_ref[i:i + 1, :]
    return mod_ref[i]


def _pre(x, g, mod_ref, sub):
    return (_rms(x) * g) * (1.0 + _mod(mod_ref, 3 * sub + 1)) + _mod(mod_ref, 3 * sub)


def _swiglu(h_ref, wu_ref, wd_ref, acc_ref):
    for c in range(N_FF_CHUNKS):
        lo, hi = c * FF_CHUNK, (c + 1) * FF_CHUNK
        h = h_ref[...]
        a = jnp.dot(h, wu_ref[:, lo:hi], preferred_element_type=F32)
        b = jnp.dot(h, wu_ref[:, D_FF + lo:D_FF + hi], preferred_element_type=F32)
        act = ((a * jax.nn.sigmoid(a)) * b).astype(BF16)
        part = jnp.dot(act, wd_ref[lo:hi, :], preferred_element_type=F32)
        if c == 0:
            acc_ref[...] = part
        else:
            acc_ref[...] += part
    return acc_ref[...]


def _ada_kernel(c_ref, w_ref, b_ref, o_ref):
    c = c_ref[...]
    h = (c * jax.nn.sigmoid(c)).astype(BF16)
    o_ref[...] = jnp.dot(h, w_ref[...].astype(BF16), preferred_element_type=F32) + b_ref[...]


def _ada_table(c_all, ada_w, ada_b):
    n = c_all.shape[0]
    n_ada = ada_w.shape[1] // D_MODEL
    return pl.pallas_call(
        _ada_kernel,
        grid=(n_ada,),
        in_specs=[
            pl.BlockSpec((n, D_MODEL), lambda i: (0, 0)),
            pl.BlockSpec((D_MODEL, D_MODEL), lambda i: (0, i)),
            pl.BlockSpec((1, D_MODEL), lambda i: (0, i)),
        ],
        out_specs=pl.BlockSpec((None, n, D_MODEL), lambda i: (i, 0, 0)),
        out_shape=jax.ShapeDtypeStruct((n_ada, n, D_MODEL), F32),
        compiler_params=pltpu.CompilerParams(dimension_semantics=("arbitrary",)),
        name="k_ada",
    )(c_all, ada_w, ada_b)


def _ffn1_front(x_ref, mod_ref, ng_ref, wu_ref, wd_ref, hb_scr, acc_scr):
    x = x_ref[...]
    hb_scr[...] = _pre(x, ng_ref[0:1, :], mod_ref, 0).astype(BF16)
    f = _swiglu(hb_scr, wu_ref, wd_ref, acc_scr)
    x1 = x + (0.5 * _mod(mod_ref, 2)) * f
    return x1, _pre(x1, ng_ref[1:2, :], mod_ref, 1)


def _ffn_qkv_prompt_kernel(x_ref, mod_ref, ng_ref, wu_ref, wd_ref, wqkv_ref, wkvt_ref,
                           x1_ref, a0_ref, a1_ref, a2_ref, kv0_ref, kv1_ref, kv2_ref,
                           hb_scr, acc_scr, h1_scr, hp_scr, *, seq_len):
    tm = x_ref.shape[0]
    i = pl.program_id(1)
    x1, h1 = _ffn1_front(x_ref, mod_ref, ng_ref, wu_ref, wd_ref, hb_scr, acc_scr)
    x1_ref[...] = x1
    for c in range(D_MODEL // LANES):
        h1_scr[c] = h1[:, c * LANES:(c + 1) * LANES]
    hb_scr[...] = h1.astype(BF16)

    for g, a_ref in enumerate((a0_ref, a1_ref, a2_ref)):
        dil = DIL_GROUPS[g][1]
        n = tm // dil
        if dil == 1:
            hp = hb_scr[...]
        else:
            for r in range(dil):
                rows = [h1_scr[c, pl.ds(r, n, stride=dil), :] for c in range(D_MODEL // LANES)]
                hp_scr[r * n:(r + 1) * n, :] = jnp.concatenate(rows, axis=1).astype(BF16)
            hp = hp_scr[...]
        for part in range(3):
            lo = part * D_ATTN + g * GROUP_W
            res = jnp.dot(hp, wqkv_ref[:, lo:lo + GROUP_W], preferred_element_type=F32)
            if part == 0:
                res = res * QK_SCALE
            a_ref[:, :, part * GROUP_W:(part + 1) * GROUP_W] = res.astype(BF16).reshape(dil, n, GROUP_W)

    for g, kv_ref in enumerate((kv0_ref, kv1_ref, kv2_ref)):
        window = DIL_GROUPS[g][0]
        first_tile = (seq_len - window) // tm

        @pl.when(i >= first_tile)
        def _(g=g, kv_ref=kv_ref, window=window):
            for part in range(2):
                lo = part * D_ATTN + g * GROUP_W
                t = lax.dot_general(wkvt_ref[lo:lo + GROUP_W, :], hb_scr[...], (((1,), (1,)), ((), ())),
                                    preferred_element_type=F32)
                if window < tm:
                    t = t[:, tm - window:]
                kv_ref[part * HEADS_PER_GROUP:(part + 1) * HEADS_PER_GROUP] = t.reshape(
                    HEADS_PER_GROUP, HEAD_DIM, t.shape[-1])


def _ffn_qkv_prompt(x, mod, norm_g, wu, wd, wqkv, wkvt):
    bsz, seq_len, _ = x.shape
    tm = ROW_TILE
    weights = (norm_g, wu, wd, wqkv, wkvt)
    a_shapes, a_specs, kv_shapes, kv_specs = [], [], [], []
    for window, dil in DIL_GROUPS:
        a_shapes.append(jax.ShapeDtypeStruct((bsz, dil, seq_len // dil, 3 * GROUP_W), BF16))
        a_specs.append(pl.BlockSpec((None, dil, tm // dil, 3 * GROUP_W), lambda b, i: (b, 0, i, 0)))
        first_tile = (seq_len - window) // tm
        kv_shapes.append(jax.ShapeDtypeStruct((bsz, 2 * HEADS_PER_GROUP, HEAD_DIM, window), F32))
        kv_specs.append(pl.BlockSpec(
            (None, 2 * HEADS_PER_GROUP, HEAD_DIM, min(window, tm)),
            lambda b, i, first_tile=first_tile: (b, 0, 0, jnp.maximum(i - first_tile, 0))))
    row_spec = pl.BlockSpec((None, tm, D_MODEL), lambda b, i: (b, i, 0))
    return pl.pallas_call(
        functools.partial(_ffn_qkv_prompt_kernel, seq_len=seq_len),
        grid=(bsz, seq_len // tm),
        in_specs=[row_spec, pl.BlockSpec((None,) + mod.shape[1:], lambda b, i: (b, 0, 0))]
        + [_resident(w.shape) for w in weights],
        out_specs=[row_spec] + a_specs + kv_specs,
        out_shape=[jax.ShapeDtypeStruct(x.shape, F32)] + a_shapes + kv_shapes,
        scratch_shapes=[pltpu.VMEM((tm, D_MODEL), BF16), pltpu.VMEM((tm, D_MODEL), F32),
                        pltpu.VMEM((D_MODEL // LANES, tm, LANES), F32), pltpu.VMEM((tm, D_MODEL), BF16)],
        compiler_params=pltpu.CompilerParams(dimension_semantics=("arbitrary", "arbitrary"),
                                             vmem_limit_bytes=VMEM_LIMIT_BYTES),
        name="k_ffn_qkv",
    )(x, mod, *weights)


def _ffn_qkv_sample_kernel(x_ref, mod_ref, ng_ref, wu_ref, wd_ref, wqkv_ref,
                           x1_ref, qkvt_ref, hb_scr, acc_scr):
    x1, h1 = _ffn1_front(x_ref, mod_ref, ng_ref, wu_ref, wd_ref, hb_scr, acc_scr)
    x1_ref[...] = x1
    qkv = jnp.dot(h1.astype(BF16), wqkv_ref[...], preferred_element_type=F32)
    col = lax.broadcasted_iota(jnp.int32, (1, 3 * D_ATTN), 1)
    qkv = qkv * jnp.where(col < D_ATTN, QK_SCALE, 1.0)
    qkvt_ref[...] = qkv.T


def _ffn_qkv_sample(x, mod, norm_g, wu, wd, wqkv):
    n = x.shape[0]
    weights = (norm_g, wu, wd, wqkv)
    return pl.pallas_call(
        _ffn_qkv_sample_kernel,
        grid=(1,),
        in_specs=[_resident(x.shape), _resident(mod.shape)] + [_resident(w.shape) for w in weights],
        out_specs=[_whole(x.shape), _whole((3 * D_ATTN, n))],
        out_shape=[jax.ShapeDtypeStruct(x.shape, F32), jax.ShapeDtypeStruct((3 * D_ATTN, n), F32)],
        scratch_shapes=[pltpu.VMEM((n, D_MODEL), BF16), pltpu.VMEM((n, D_MODEL), F32)],
        compiler_params=pltpu.CompilerParams(dimension_semantics=("arbitrary",),
                                             vmem_limit_bytes=VMEM_LIMIT_BYTES),
        name="k_ffn_qkv_sample",
    )(x, mod, *weights)


def _window_kernel(cur_ref, prev_ref, og_ref, kv_scr, *, blocks_per_seq):
    tbb = cur_ref.shape[0]
    nq = KEYS_PER_QUERY_BLOCK
    j = pl.program_id(0)
    kv_scr[0:nq, :] = prev_ref[0, :, GROUP_W:]
    for t in range(tbb):
        kv_scr[(t + 1) * nq:(t + 2) * nq, :] = cur_ref[t, :, GROUP_W:]

    lane_head = lax.broadcasted_iota(jnp.int32, (nq, GROUP_W), 1) // HEAD_DIM
    qi = lax.broadcasted_iota(jnp.int32, (HEADS_PER_GROUP * nq, 2 * nq), 0) % nq
    kj = lax.broadcasted_iota(jnp.int32, (HEADS_PER_GROUP * nq, 2 * nq), 1)
    band_key = jnp.where((kj >= qi) & (kj <= qi + nq), kj, -1)

    def body(t, carry):
        q = cur_ref[t, :, 0:GROUP_W]
        start = pl.multiple_of(t * nq, nq)
        k2 = kv_scr[pl.ds(start, 2 * nq), 0:GROUP_W]
        v2 = kv_scr[pl.ds(start, 2 * nq), GROUP_W:]
        qs = jnp.concatenate([jnp.where(lane_head == h, q, jnp.zeros_like(q))
                              for h in range(HEADS_PER_GROUP)], axis=0)
        s = lax.dot_general(qs, k2, (((1,), (1,)), ((), ())), preferred_element_type=F32)
        first_key = jnp.where(((j * tbb + t) % blocks_per_seq) == 0, nq, 0)
        s = jnp.where(band_key >= first_key, s, MASKED)
        m = jnp.max(s, axis=1, keepdims=True)
        p = jnp.exp(s - m)
        l = jnp.sum(p, axis=1, keepdims=True)
        r = jnp.dot(p.astype(BF16), v2, preferred_element_type=F32)
        inv_l = 1.0 / l
        lse = m + jnp.log(l)
        o = jnp.zeros((nq, GROUP_W), F32)
        lse_b = jnp.zeros((nq, GROUP_W), F32)
        for h in range(HEADS_PER_GROUP):
            rows = slice(h * nq, (h + 1) * nq)
            o = jnp.where(lane_head == h, r[rows] * inv_l[rows], o)
            lse_b = jnp.where(lane_head == h, lse[rows], lse_b)
        og_ref[t, :, 0:GROUP_W] = o
        og_ref[t, :, GROUP_W:] = lse_b
        return carry

    lax.fori_loop(0, tbb, body, 0, unroll=WINDOW_UNROLL)


def _window_attention(a_blocks, blocks_per_seq):
    n_blocks = a_blocks.shape[0]
    tbb = WINDOW_BLOCKS_PER_STEP
    nq = KEYS_PER_QUERY_BLOCK
    return pl.pallas_call(
        functools.partial(_window_kernel, blocks_per_seq=blocks_per_seq),
        grid=(n_blocks // tbb,),
        in_specs=[pl.BlockSpec((tbb, nq, 3 * GROUP_W), lambda j: (j, 0, 0)),
                  pl.BlockSpec((1, nq, 3 * GROUP_W), lambda j: (jnp.maximum(j * tbb - 1, 0), 0, 0))],
        out_specs=pl.BlockSpec((tbb, nq, 2 * GROUP_W), lambda j: (j, 0, 0)),
        out_shape=jax.ShapeDtypeStruct((n_blocks, nq, 2 * GROUP_W), F32),
        scratch_shapes=[pltpu.VMEM(((tbb + 1) * nq, 2 * GROUP_W), BF16)],
        compiler_params=pltpu.CompilerParams(dimension_semantics=("arbitrary",)),
        name="k_window",
    )(a_blocks, a_blocks)


def _merge_groups(o_and_lse):
    m = o_and_lse[0][1]
    for _, lse in o_and_lse[1:]:
        m = jnp.maximum(m, lse)
    es = [jnp.exp(lse - m) for _, lse in o_and_lse]
    inv = 1.0 / (es[0] + es[1] + es[2])
    return jnp.concatenate([(o * (e * inv)).astype(BF16) for (o, _), e in zip(o_and_lse, es)], axis=1)


def _mix_tail(x1, h1, u, mixed, attn, mod_ref, ng_ref, wr_ref, wba_ref, wbb_ref, wo_ref,
              wu_ref, wd_ref, fg_ref, hb_scr, acc_scr):
    gm = (u * mixed).astype(BF16)
    gg = jnp.dot(h1, wr_ref[:, 2 * D_GMLP:], preferred_element_type=F32)
    merged = (jax.nn.sigmoid(gg[:, :D_MODEL]) * jnp.dot(attn, wba_ref[...], preferred_element_type=F32)
              + jax.nn.sigmoid(gg[:, D_MODEL:]) * jnp.dot(gm, wbb_ref[...], preferred_element_type=F32))
    y = jnp.dot(merged.astype(BF16), wo_ref[...], preferred_element_type=F32)
    x2 = x1 + _mod(mod_ref, 5) * y
    hb_scr[...] = _pre(x2, ng_ref[2:3, :], mod_ref, 2).astype(BF16)
    f = _swiglu(hb_scr, wu_ref, wd_ref, acc_scr)
    x3 = x2 + (0.5 * _mod(mod_ref, 8)) * f
    return _rms(x3) * fg_ref[...]


def _layer_norm_rows(v, g_ref, b_ref):
    mu = jnp.mean(v, axis=-1, keepdims=True)
    var = jnp.mean(jnp.square(v - mu), axis=-1, keepdims=True)
    return ((v - mu) * lax.rsqrt(var + EPS)) * g_ref[...] + b_ref[...]


def _mix_ffn_prompt_kernel(x1_ref, og0_ref, og1_ref, og2_ref, mod_ref, ng_ref, wr_ref, lng_ref, lnb_ref,
                           ws_ref, bst_ref, wba_ref, wbb_ref, wo_ref, wu_ref, wd_ref, fg_ref,
                           y_ref, vrows_ref, hb_scr, acc_scr, un_scr, mixed_scr):
    tm = x1_ref.shape[0]
    i = pl.program_id(1)
    x1 = x1_ref[...]
    h1 = _pre(x1, ng_ref[1:2, :], mod_ref, 1).astype(BF16)
    uv = jnp.dot(h1, wr_ref[:, :2 * D_GMLP], preferred_element_type=F32)
    u = uv[:, :D_GMLP]
    vn = _layer_norm_rows(uv[:, D_GMLP:], lng_ref, lnb_ref)

    @pl.when(i == pl.num_programs(1) - 1)
    def _():
        vrows_ref[...] = vn[tm - CHUNK:, :]

    vnb = vn.astype(BF16)
    tri = (lax.broadcasted_iota(jnp.int32, (CHUNK, CHUNK), 0)
           >= lax.broadcasted_iota(jnp.int32, (CHUNK, CHUNK), 1))
    gw = D_GMLP // GMLP_GROUPS
    for g in range(GMLP_GROUPS):
        ws = jnp.where(tri, ws_ref[g], 0.0).astype(BF16)
        bias = bst_ref[:, g:g + 1]
        for c in range(tm // CHUNK):
            mixed_scr[c * CHUNK:(c + 1) * CHUNK, g * gw:(g + 1) * gw] = (
                jnp.dot(ws, vnb[c * CHUNK:(c + 1) * CHUNK, g * gw:(g + 1) * gw],
                        preferred_element_type=F32) + bias)

    for g, og_ref in enumerate((og0_ref, og1_ref, og2_ref)):
        dil = DIL_GROUPS[g][1]
        n = tm // dil
        for r in range(dil):
            for c in range(2 * GROUP_W // LANES):
                un_scr[g, c, pl.ds(r, n, stride=dil), :] = og_ref[r, :, c * LANES:(c + 1) * LANES]
    o_and_lse = []
    half = GROUP_W // LANES
    for g in range(N_GROUPS):
        cols = [un_scr[g, c] for c in range(2 * half)]
        o_and_lse.append((jnp.concatenate(cols[:half], axis=1), jnp.concatenate(cols[half:], axis=1)))
    attn = _merge_groups(o_and_lse)

    y_ref[...] = _mix_tail(x1, h1, u, mixed_scr[...], attn, mod_ref, ng_ref, wr_ref, wba_ref, wbb_ref, wo_ref,
                           wu_ref, wd_ref, fg_ref, hb_scr, acc_scr)


def _mix_ffn_prompt(x1, ogs, mod, norm_g, wr, lng, lnb, ws, bst, wba, wbb, wo, wu, wd, fg):
    bsz, seq_len, _ = x1.shape
    tm = ROW_TILE
    weights = (norm_g, wr, lng, lnb, ws, bst, wba, wbb, wo, wu, wd, fg)
    row_spec = pl.BlockSpec((None, tm, D_MODEL), lambda b, i: (b, i, 0))
    og_specs = [pl.BlockSpec((None, dil, tm // dil, 2 * GROUP_W), lambda b, i: (b, 0, i, 0))
                for _, dil in DIL_GROUPS]
    return pl.pallas_call(
        _mix_ffn_prompt_kernel,
        grid=(bsz, seq_len // tm),
        in_specs=[row_spec] + og_specs + [pl.BlockSpec((None,) + mod.shape[1:], lambda b, i: (b, 0, 0))]
        + [_resident(w.shape) for w in weights],
        out_specs=[row_spec, pl.BlockSpec((None, CHUNK, D_GMLP), lambda b, i: (b, 0, 0))],
        out_shape=[jax.ShapeDtypeStruct(x1.shape, F32), jax.ShapeDtypeStruct((bsz, CHUNK, D_GMLP), F32)],
        scratch_shapes=[pltpu.VMEM((tm, D_MODEL), BF16), pltpu.VMEM((tm, D_MODEL), F32),
                        pltpu.VMEM((N_GROUPS, 2 * GROUP_W // LANES, tm, LANES), F32),
                        pltpu.VMEM((tm, D_GMLP), F32)],
        compiler_params=pltpu.CompilerParams(dimension_semantics=("arbitrary", "arbitrary"),
                                             vmem_limit_bytes=VMEM_LIMIT_BYTES),
        name="k_mix_ffn",
    )(x1, *ogs, mod, *weights)


def _mix_ffn_sample_kernel(x1_ref, attnt_ref, mod_ref, ng_ref, wr_ref, lng_ref, lnb_ref,
                           ws0_ref, bs0_ref, wba_ref, wbb_ref, wo_ref, wu_ref, wd_ref, fg_ref,
                           y_ref, vrows_ref, hb_scr, acc_scr):
    x1 = x1_ref[...]
    h1 = _pre(x1, ng_ref[1:2, :], mod_ref, 1).astype(BF16)
    uv = jnp.dot(h1, wr_ref[:, :2 * D_GMLP], preferred_element_type=F32)
    u = uv[:, :D_GMLP]
    vn = _layer_norm_rows(uv[:, D_GMLP:], lng_ref, lnb_ref)
    vrows_ref[...] = vn
    mixed = ws0_ref[...] * vn + bs0_ref[...]
    attn = attnt_ref[...].T.astype(BF16)
    y_ref[...] = _mix_tail(x1, h1, u, mixed, attn, mod_ref, ng_ref, wr_ref, wba_ref, wbb_ref, wo_ref,
                           wu_ref, wd_ref, fg_ref, hb_scr, acc_scr)


def _mix_ffn_sample(x1, attnt, mod, norm_g, wr, lng, lnb, ws0, bs0, wba, wbb, wo, wu, wd, fg):
    n = x1.shape[0]
    operands = (x1, attnt, mod, norm_g, wr, lng, lnb, ws0, bs0, wba, wbb, wo, wu, wd, fg)
    return pl.pallas_call(
        _mix_ffn_sample_kernel,
        grid=(1,),
        in_specs=[_resident(w.shape) for w in operands],
        out_specs=[_whole(x1.shape), _whole((n, D_GMLP))],
        out_shape=[jax.ShapeDtypeStruct(x1.shape, F32), jax.ShapeDtypeStruct((n, D_GMLP), F32)],
        scratch_shapes=[pltpu.VMEM((n, D_MODEL), BF16), pltpu.VMEM((n, D_MODEL), F32)],
        compiler_params=pltpu.CompilerParams(dimension_semantics=("arbitrary",),
                                             vmem_limit_bytes=VMEM_LIMIT_BYTES),
        name="k_mix_ffn_sample",
    )(*operands)


def _sample_attn_kernel(qkvt_ref, c0_ref, c1_ref, c2_ref, o0_ref, o1_ref, o2_ref, attnt_ref):
    b = pl.program_id(0)
    n_req = qkvt_ref.shape[1]
    onehot = (lax.broadcasted_iota(jnp.int32, (1, n_req), 1) == b).astype(F32)
    cols = jnp.sum(qkvt_ref[...] * onehot, axis=1, keepdims=True)

    @pl.when(b == 0)
    def _():
        attnt_ref[...] = jnp.zeros_like(attnt_ref)

    outs = []
    for g, (c_ref, o_ref) in enumerate(((c0_ref, o0_ref), (c1_ref, o1_ref), (c2_ref, o2_ref))):
        window, dil = DIL_GROUPS[g]
        rows = c_ref.shape[-1]
        col = lax.broadcasted_iota(jnp.int32, (1, rows), 1)
        dist = rows - col
        strided = (dist % dil == 0) & (dist <= window)
        newest = col == rows - 1
        for h in range(HEADS_PER_GROUP):
            base = g * GROUP_W + h * HEAD_DIM
            qc = cols[base:base + HEAD_DIM]
            kc = cols[D_ATTN + base:D_ATTN + base + HEAD_DIM]
            vc = cols[2 * D_ATTN + base:2 * D_ATTN + base + HEAD_DIM]
            kt = c_ref[h]
            vt = c_ref[HEADS_PER_GROUP + h]
            s = jnp.sum(kt * qc, axis=0, keepdims=True)
            s = jnp.where(strided, s, MASKED)
            s_new = jnp.sum(kc * qc, axis=0, keepdims=True)
            m = jnp.maximum(jnp.max(s, axis=1, keepdims=True), s_new)
            p = jnp.exp(s - m)
            p_new = jnp.exp(s_new - m)
            l = jnp.sum(p, axis=1, keepdims=True) + p_new
            o = (jnp.sum(vt * p, axis=1, keepdims=True) + vc * p_new) / l
            outs.append((base, o, m + jnp.log(l)))
            o_ref[h] = jnp.where(newest, kc, pltpu.roll(kt, rows - 1, axis=1))
            o_ref[HEADS_PER_GROUP + h] = jnp.where(newest, vc, pltpu.roll(vt, rows - 1, axis=1))

    for h in range(HEADS_PER_GROUP):
        per_group = [outs[g * HEADS_PER_GROUP + h] for g in range(N_GROUPS)]
        m = jnp.maximum(jnp.maximum(per_group[0][2], per_group[1][2]), per_group[2][2])
        es = [jnp.exp(lse - m) for _, _, lse in per_group]
        inv = 1.0 / (es[0] + es[1] + es[2])
        for (base, o, _), e in zip(per_group, es):
            attnt_ref[base:base + HEAD_DIM, :] += (o * (e * inv)) * onehot


def _sample_attention(qkvt, caches):
    n_req = qkvt.shape[1]
    c_specs = [pl.BlockSpec((None,) + c.shape[1:], lambda b: (b, 0, 0, 0)) for c in caches]
    return pl.pallas_call(
        _sample_attn_kernel,
        grid=(n_req,),
        in_specs=[_resident(qkvt.shape)] + c_specs,
        out_specs=c_specs + [pl.BlockSpec((D_ATTN, n_req), lambda b: (0, 0))],
        out_shape=[jax.ShapeDtypeStruct(c.shape, F32) for c in caches]
        + [jax.ShapeDtypeStruct((D_ATTN, n_req), F32)],
        compiler_params=pltpu.CompilerParams(dimension_semantics=("arbitrary",),
                                             vmem_limit_bytes=VMEM_LIMIT_BYTES),
        name="k_sample_attn",
    )(qkvt, *caches)


def _to_buffer_layout(cache):
    n, rows = cache.shape[0], cache.shape[1]
    return cache.transpose(0, 2, 3, 4, 1).reshape(n, 2 * HEADS_PER_GROUP, HEAD_DIM, rows)


def _from_buffer_layout(buf):
    n, rows = buf.shape[0], buf.shape[-1]
    return buf.reshape(n, 2, HEADS_PER_GROUP, HEAD_DIM, rows).transpose(0, 4, 1, 2, 3)[None]


def kernel(x_prompt, x_sample, c_prompt, c_sample, cache_kv_g0, cache_kv_g1, cache_kv_g2, ada_w, ada_b, norm_g,
           ffn1_up, ffn1_down, w_in, w_branch_a, w_branch_b, w_out, v_ln_g, v_ln_b, w_spatial, b_spatial,
           ffn2_up, ffn2_down, final_g):
    assert ada_w.shape[0] == 1, "one layer"
    bsz, seq_len, _ = x_prompt.shape
    n_req = x_sample.shape[0]

    wu1, wd1 = ffn1_up[0].astype(BF16), ffn1_down[0].astype(BF16)
    wu2, wd2 = ffn2_up[0].astype(BF16), ffn2_down[0].astype(BF16)
    wqkv = w_in[0][:, :3 * D_ATTN].astype(BF16)
    wr = w_in[0][:, 3 * D_ATTN:].astype(BF16)
    wkvt = wqkv[:, D_ATTN:].T
    wba, wbb, wo = w_branch_a[0].astype(BF16), w_branch_b[0].astype(BF16), w_out[0].astype(BF16)
    ng = norm_g[0]
    lng, lnb = v_ln_g, v_ln_b
    ws = w_spatial[0]
    bst = b_spatial[0].T
    gw = D_GMLP // GMLP_GROUPS
    ws0 = jnp.repeat(ws[:, 0, 0], gw)[None, :]
    bs0 = jnp.repeat(b_spatial[0][:, 0], gw)[None, :]
    fg = final_g[None, :]

    mods = _ada_table(jnp.concatenate([c_prompt, c_sample], axis=0), ada_w[0], ada_b)
    mod_p = mods[:, :bsz].transpose(1, 0, 2)
    mod_s = mods[:, bsz:]

    x1, a0, a1, a2, kv0, kv1, kv2 = _ffn_qkv_prompt(x_prompt, mod_p, ng, wu1, wd1, wqkv, wkvt)
    ogs = []
    for a, (_, dil) in zip((a0, a1, a2), DIL_GROUPS):
        blocks = a.reshape(-1, KEYS_PER_QUERY_BLOCK, 3 * GROUP_W)
        og = _window_attention(blocks, seq_len // dil // KEYS_PER_QUERY_BLOCK)
        ogs.append(og.reshape(bsz, dil, seq_len // dil, 2 * GROUP_W))
    y_prompt, vrows_p = _mix_ffn_prompt(x1, ogs, mod_p, ng, wr, lng, lnb, ws, bst, wba, wbb, wo,
                                        wu2, wd2, fg)

    xs = x_sample.reshape(n_req, D_MODEL)
    x1s, qkvt = _ffn_qkv_sample(xs, mod_s, ng, wu1, wd1, wqkv)
    caches = [_to_buffer_layout(c[0]) for c in (cache_kv_g0, cache_kv_g1, cache_kv_g2)]
    s0, s1, s2, attnt = _sample_attention(qkvt, caches)
    y_sample, vrows_s = _mix_ffn_sample(x1s, attnt, mod_s, ng, wr, lng, lnb, ws0, bs0, wba, wbb, wo,
                                        wu2, wd2, fg)

    return (y_prompt, y_sample.reshape(n_req, 1, D_MODEL),
            _from_buffer_layout(kv0), _from_buffer_layout(kv1), _from_buffer_layout(kv2), vrows_p[None],
            _from_buffer_layout(s0), _from_buffer_layout(s1), _from_buffer_layout(s2),
            vrows_s.reshape(1, n_req, 1, D_GMLP))
```

```python
import functools

import jax
import jax.numpy as jnp
from jax import lax
from jax.experimental import pallas as pl
from jax.experimental.pallas import tpu as pltpu

F32 = jnp.float32
BF16 = jnp.bfloat16

D_MODEL = 1024
HEAD_DIM = 64
HEADS_PER_GROUP = 4
GROUP_W = HEADS_PER_GROUP * HEAD_DIM
DIL_GROUPS = ((128, 1), (512, 4), (2048, 16))
N_GROUPS = len(DIL_GROUPS)
D_ATTN = N_GROUPS * GROUP_W
D_GMLP = 512
GMLP_GROUPS = 4
CHUNK = 128
D_FF = 2816
FF_CHUNK = 256
N_FF_CHUNKS = D_FF // FF_CHUNK
EPS = 1e-6
QK_SCALE = HEAD_DIM ** -0.5
KEYS_PER_QUERY_BLOCK = 128
MASKED = -1e30
LANES = 128

VMEM_LIMIT_BYTES = 60 * 1024 * 1024
ROW_TILE = 512


def _resident(shape):
    zeros = (0,) * len(shape)
    return pl.BlockSpec(shape, lambda *_: zeros, pipeline_mode=pl.Buffered(1))


def _whole(shape):
    zeros = (0,) * len(shape)
    return pl.BlockSpec(shape, lambda *_: zeros)


def _rms(x):
    return x * lax.rsqrt(jnp.mean(x * x, axis=-1, keepdims=True) + EPS)


def _mod(mod_ref, i):
    if len(mod_ref.shape) == 2:
        return mod_ref[i:i + 1, :]
    return mod_ref[i]


def _pre(x, g, mod_ref, sub):
    return (_rms(x) * g) * (1.0 + _mod(mod_ref, 3 * sub + 1)) + _mod(mod_ref, 3 * sub)


def _swiglu(h_ref, wu_ref, wd_ref, acc_ref):
    for c in range(N_FF_CHUNKS):
        lo, hi = c * FF_CHUNK, (c + 1) * FF_CHUNK
        h = h_ref[...]
        a = jnp.dot(h, wu_ref[:, lo:hi], preferred_element_type=F32)
        b = jnp.dot(h, wu_ref[:, D_FF + lo:D_FF + hi], preferred_element_type=F32)
        act = ((a * jax.nn.sigmoid(a)) * b).astype(BF16)
        part = jnp.dot(act, wd_ref[lo:hi, :], preferred_element_type=F32)
        if c == 0:
            acc_ref[...] = part
        else:
            acc_ref[...] += part
    return acc_ref[...]


def _ada_kernel(c_ref, w_ref, b_ref, o_ref):
    c = c_ref[...]
    h = (c * jax.nn.sigmoid(c)).astype(BF16)
    o_ref[...] = jnp.dot(h, w_ref[...].astype(BF16), preferred_element_type=F32) + b_ref[...]


def _ada_table(c_all, ada_w, ada_b):
    n = c_all.shape[0]
    n_ada = ada_w.shape[1] // D_MODEL
    return pl.pallas_call(
        _ada_kernel,
        grid=(n_ada,),
        in_specs=[
            pl.BlockSpec((n, D_MODEL), lambda i: (0, 0)),
            pl.BlockSpec((D_MODEL, D_MODEL), lambda i: (0, i)),
            pl.BlockSpec((1, D_MODEL), lambda i: (0, i)),
        ],
        out_specs=pl.BlockSpec((None, n, D_MODEL), lambda i: (i, 0, 0)),
        out_shape=jax.ShapeDtypeStruct((n_ada, n, D_MODEL), F32),
        compiler_params=pltpu.CompilerParams(dimension_semantics=("arbitrary",)),
        name="k_ada",
    )(c_all, ada_w, ada_b)


def _ffn1_front(x_ref, mod_ref, ng_ref, wu_ref, wd_ref, hb_scr, acc_scr):
    x = x_ref[...]
    hb_scr[...] = _pre(x, ng_ref[0:1, :], mod_ref, 0).astype(BF16)
    f = _swiglu(hb_scr, wu_ref, wd_ref, acc_scr)
    x1 = x + (0.5 * _mod(mod_ref, 2)) * f
    return x1, _pre(x1, ng_ref[1:2, :], mod_ref, 1)


def _ffn_qkv_prompt_kernel(x_ref, mod_ref, ng_ref, wu_ref, wd_ref, wqkv_ref, wkvt_ref,
                           x1_ref, a0_ref, a1_ref, a2_ref, kv0_ref, kv1_ref, kv2_ref,
                           hb_scr, acc_scr, h1_scr, hp_scr, *, seq_len):
    tm = x_ref.shape[0]
    i = pl.program_id(1)
    x1, h1 = _ffn1_front(x_ref, mod_ref, ng_ref, wu_ref, wd_ref, hb_scr, acc_scr)
    x1_ref[...] = x1
    for c in range(D_MODEL // LANES):
        h1_scr[c] = h1[:, c * LANES:(c + 1) * LANES]
    hb_scr[...] = h1.astype(BF16)

    for g, a_ref in enumerate((a0_ref, a1_ref, a2_ref)):
        dil = DIL_GROUPS[g][1]
        n = tm // dil
        if dil == 1:
            hp = hb_scr[...]
        else:
            for r in range(dil):
                rows = [h1_scr[c, pl.ds(r, n, stride=dil), :] for c in range(D_MODEL // LANES)]
                hp_scr[r * n:(r + 1) * n, :] = jnp.concatenate(rows, axis=1).astype(BF16)
            hp = hp_scr[...]
        for part in range(3):
            lo = part * D_ATTN + g * GROUP_W
            res = jnp.dot(hp, wqkv_ref[:, lo:lo + GROUP_W], preferred_element_type=F32)
            if part == 0:
                res = res * QK_SCALE
            a_ref[:, :, part * GROUP_W:(part + 1) * GROUP_W] = res.astype(BF16).reshape(dil, n, GROUP_W)

    for g, kv_ref in enumerate((kv0_ref, kv1_ref, kv2_ref)):
        window = DIL_GROUPS[g][0]
        first_tile = (seq_len - window) // tm

        @pl.when(i >= first_tile)
        def _(g=g, kv_ref=kv_ref, window=window):
            for part in range(2):
                lo = part * D_ATTN + g * GROUP_W
                t = lax.dot_general(wkvt_ref[lo:lo + GROUP_W, :], hb_scr[...], (((1,), (1,)), ((), ())),
                                    preferred_element_type=F32)
                if window < tm:
                    t = t[:, tm - window:]
                kv_ref[part * HEADS_PER_GROUP:(part + 1) * HEADS_PER_GROUP] = t.reshape(
                    HEADS_PER_GROUP, HEAD_DIM, t.shape[-1])


def _ffn_qkv_prompt(x, mod, norm_g, wu, wd, wqkv, wkvt):
    bsz, seq_len, _ = x.shape
    tm = ROW_TILE
    weights = (norm_g, wu, wd, wqkv, wkvt)
    a_shapes, a_specs, kv_shapes, kv_specs = [], [], [], []
    for window, dil in DIL_GROUPS:
        a_shapes.append(jax.ShapeDtypeStruct((bsz, dil, seq_len // dil, 3 * GROUP_W), BF16))
        a_specs.append(pl.BlockSpec((None, dil, tm // dil, 3 * GROUP_W), lambda b, i: (b, 0, i, 0)))
        first_tile = (seq_len - window) // tm
        kv_shapes.append(jax.ShapeDtypeStruct((bsz, 2 * HEADS_PER_GROUP, HEAD_DIM, window), F32))
        kv_specs.append(pl.BlockSpec(
            (None, 2 * HEADS_PER_GROUP, HEAD_DIM, min(window, tm)),
            lambda b, i, first_tile=first_tile: (b, 0, 0, jnp.maximum(i - first_tile, 0))))
    row_spec = pl.BlockSpec((None, tm, D_MODEL), lambda b, i: (b, i, 0))
    return pl.pallas_call(
        functools.partial(_ffn_qkv_prompt_kernel, seq_len=seq_len),
        grid=(bsz, seq_len // tm),
        in_specs=[row_spec, pl.BlockSpec((None,) + mod.shape[1:], lambda b, i: (b, 0, 0))]
        + [_resident(w.shape) for w in weights],
        out_specs=[row_spec] + a_specs + kv_specs,
        out_shape=[jax.ShapeDtypeStruct(x.shape, F32)] + a_shapes + kv_shapes,
        scratch_shapes=[pltpu.VMEM((tm, D_MODEL), BF16), pltpu.VMEM((tm, D_MODEL), F32),
                        pltpu.VMEM((D_MODEL // LANES, tm, LANES), F32), pltpu.VMEM((tm, D_MODEL), BF16)],
        compiler_params=pltpu.CompilerParams(dimension_semantics=("arbitrary", "arbitrary"),
                                             vmem_limit_bytes=VMEM_LIMIT_BYTES),
        name="k_ffn_qkv",
    )(x, mod, *weights)


def _ffn_qkv_sample_kernel(x_ref, mod_ref, ng_ref, wu_ref, wd_ref, wqkv_ref,
                           x1_ref, qkvt_ref, hb_scr, acc_scr):
    x1, h1 = _ffn1_front(x_ref, mod_ref, ng_ref, wu_ref, wd_ref, hb_scr, acc_scr)
    x1_ref[...] = x1
    qkv = jnp.dot(h1.astype(BF16), wqkv_ref[...], preferred_element_type=F32)
    col = lax.broadcasted_iota(jnp.int32, (1, 3 * D_ATTN), 1)
    qkv = qkv * jnp.where(col < D_ATTN, QK_SCALE, 1.0)
    qkvt_ref[...] = qkv.T


def _ffn_qkv_sample(x, mod, norm_g, wu, wd, wqkv):
    n = x.shape[0]
    weights = (norm_g, wu, wd, wqkv)
    return pl.pallas_call(
        _ffn_qkv_sample_kernel,
        grid=(1,),
        in_specs=[_resident(x.shape), _resident(mod.shape)] + [_resident(w.shape) for w in weights],
        out_specs=[_whole(x.shape), _whole((3 * D_ATTN, n))],
        out_shape=[jax.ShapeDtypeStruct(x.shape, F32), jax.ShapeDtypeStruct((3 * D_ATTN, n), F32)],
        scratch_shapes=[pltpu.VMEM((n, D_MODEL), BF16), pltpu.VMEM((n, D_MODEL), F32)],
        compiler_params=pltpu.CompilerParams(dimension_semantics=("arbitrary",),
                                             vmem_limit_bytes=VMEM_LIMIT_BYTES),
        name="k_ffn_qkv_sample",
    )(x, mod, *weights)


def _band_block(q, k2, v2, first_key, band_key, lane_head, ones):
    nq = KEYS_PER_QUERY_BLOCK
    qs = jnp.concatenate([jnp.where(lane_head == h, q, jnp.zeros_like(q)) for h in range(HEADS_PER_GROUP)], axis=0)
    s = lax.dot_general(qs, k2, (((1,), (1,)), ((), ())), preferred_element_type=F32)
    s = jnp.where(band_key >= first_key, s, MASKED)
    m = jnp.max(s, axis=1, keepdims=True)
    p = jnp.exp(s - m).astype(BF16)
    r = jnp.dot(p, v2, preferred_element_type=F32)
    l = jnp.dot(p, ones, preferred_element_type=F32)
    inv_l = 1.0 / l
    lse = m + jnp.log(l)
    o = lse_b = None
    for h in range(HEADS_PER_GROUP):
        rows = slice(h * nq, (h + 1) * nq)
        o_h = r[rows] * jnp.concatenate([inv_l[rows]] * (GROUP_W // LANES), axis=1)
        lse_h = jnp.concatenate([lse[rows]] * (GROUP_W // LANES), axis=1)
        o = o_h if h == 0 else jnp.where(lane_head == h, o_h, o)
        lse_b = lse_h if h == 0 else jnp.where(lane_head == h, lse_h, lse_b)
    return o, lse_b


def _shifted(buf, newest, new_col):
    return jnp.where(newest, new_col, pltpu.roll(buf, buf.shape[-1] - 1, axis=1))


def _sample_scores(g, h, cols, c_ref, n_ref):
    window, dil = DIL_GROUPS[g]
    rows = c_ref.shape[-1]
    col = lax.broadcasted_iota(jnp.int32, (1, rows), 1)
    dist = rows - col
    strided = (dist % dil == 0) & (dist <= window)
    base = g * GROUP_W + h * HEAD_DIM
    qc = cols[base:base + HEAD_DIM]
    kc = cols[D_ATTN + base:D_ATTN + base + HEAD_DIM]
    kt = c_ref[h]
    s = jnp.where(strided, jnp.sum(kt * qc, axis=0, keepdims=True), MASKED)
    s_new = jnp.sum(kc * qc, axis=0, keepdims=True)
    m = jnp.maximum(jnp.max(s, axis=1, keepdims=True), s_new)
    p = jnp.exp(s - m)
    p_new = jnp.exp(s_new - m)
    l = jnp.sum(p, axis=1, keepdims=True) + p_new
    n_ref[h] = _shifted(kt, col == rows - 1, kc)
    return p, p_new, l, m + jnp.log(l)


def _sample_values(g, h, cols, c_ref, n_ref, p, p_new, l):
    rows = c_ref.shape[-1]
    col = lax.broadcasted_iota(jnp.int32, (1, rows), 1)
    base = 2 * D_ATTN + g * GROUP_W + h * HEAD_DIM
    vc = cols[base:base + HEAD_DIM]
    vt = c_ref[HEADS_PER_GROUP + h]
    n_ref[HEADS_PER_GROUP + h] = _shifted(vt, col == rows - 1, vc)
    return (jnp.sum(vt * p, axis=1, keepdims=True) + vc * p_new) / l


def _attn_kernel(qkvt_ref, a0_ref, p0_ref, a1_ref, p1_ref, a2_ref, p2_ref, c0_ref, c1_ref, c2_ref,
                 og0_ref, og1_ref, og2_ref, n0_ref, n1_ref, n2_ref, attnt_ref, *, blocks_per_seq):
    step = pl.program_id(0)
    nq = KEYS_PER_QUERY_BLOCK

    @pl.when(step == 0)
    def _():
        attnt_ref[...] = jnp.zeros_like(attnt_ref)

    lane_head = lax.broadcasted_iota(jnp.int32, (nq, GROUP_W), 1) // HEAD_DIM
    qi = lax.broadcasted_iota(jnp.int32, (HEADS_PER_GROUP * nq, 2 * nq), 0) % nq
    kj = lax.broadcasted_iota(jnp.int32, (HEADS_PER_GROUP * nq, 2 * nq), 1)
    band_key = jnp.where((kj >= qi) & (kj <= qi + nq), kj, -1)
    ones = jnp.ones((2 * nq, LANES), BF16)

    def band_task(g, t, a_ref, p_ref, og_ref):
        tbb = a_ref.shape[0]
        before = p_ref[0] if t == 0 else a_ref[t - 1]
        own = a_ref[t]
        k2 = jnp.concatenate([before[:, GROUP_W:2 * GROUP_W], own[:, GROUP_W:2 * GROUP_W]], axis=0)
        v2 = jnp.concatenate([before[:, 2 * GROUP_W:], own[:, 2 * GROUP_W:]], axis=0)
        first_key = jnp.where(((step * tbb + t) % blocks_per_seq[g]) == 0, nq, 0)
        o, lse_b = _band_block(own[:, 0:GROUP_W], k2, v2, first_key, band_key, lane_head, ones)
        og_ref[t, :, 0:GROUP_W] = o
        og_ref[t, :, GROUP_W:] = lse_b

    band_tasks = [functools.partial(band_task, g, t, a_ref, p_ref, og_ref)
                  for g, (a_ref, p_ref, og_ref) in enumerate(((a0_ref, p0_ref, og0_ref), (a1_ref, p1_ref, og1_ref),
                                                              (a2_ref, p2_ref, og2_ref)))
                  for t in range(a_ref.shape[0])]

    n_req = qkvt_ref.shape[1]
    onehot = (lax.broadcasted_iota(jnp.int32, (1, n_req), 1) == step).astype(F32)
    cols = jnp.sum(qkvt_ref[...] * onehot, axis=1, keepdims=True)
    heads = {}

    def scores_task(g, h, c_ref, n_ref):
        heads[g, h] = _sample_scores(g, h, cols, c_ref, n_ref)

    def values_task(g, h, c_ref, n_ref):
        p, p_new, l, lse = heads[g, h]
        heads[g, h] = (_sample_values(g, h, cols, c_ref, n_ref, p, p_new, l), lse)

    sample_tasks = []
    for g, (c_ref, n_ref) in enumerate(((c0_ref, n0_ref), (c1_ref, n1_ref), (c2_ref, n2_ref))):
        for h in range(HEADS_PER_GROUP):
            sample_tasks.append((c_ref.shape[-1], functools.partial(scores_task, g, h, c_ref, n_ref)))
            sample_tasks.append((c_ref.shape[-1], functools.partial(values_task, g, h, c_ref, n_ref)))

    total = sum(cost for cost, _ in sample_tasks)
    dealt = nxt = 0
    for i, task in enumerate(band_tasks):
        task()
        while nxt < len(sample_tasks) and dealt * len(band_tasks) < total * (i + 1):
            cost, sample_task = sample_tasks[nxt]
            sample_task()
            dealt += cost
            nxt += 1
    for _, sample_task in sample_tasks[nxt:]:
        sample_task()

    for h in range(HEADS_PER_GROUP):
        lses = [heads[g, h][1] for g in range(N_GROUPS)]
        m = jnp.maximum(jnp.maximum(lses[0], lses[1]), lses[2])
        es = [jnp.exp(lse - m) for lse in lses]
        inv = 1.0 / (es[0] + es[1] + es[2])
        for g in range(N_GROUPS):
            base = g * GROUP_W + h * HEAD_DIM
            attnt_ref[base:base + HEAD_DIM, :] += (heads[g, h][0] * (es[g] * inv)) * onehot


def _attention(qkvt, a_blocks, caches, blocks_per_seq):
    n_req = qkvt.shape[1]
    n_blocks = a_blocks[0].shape[0]
    assert n_blocks % n_req == 0, "query blocks are dealt evenly over the per-request grid steps"
    tbb = n_blocks // n_req
    nq = KEYS_PER_QUERY_BLOCK
    a_specs, operands = [], [qkvt]
    for a in a_blocks:
        a_specs += [pl.BlockSpec((tbb, nq, 3 * GROUP_W), lambda r: (r, 0, 0)),
                    pl.BlockSpec((1, nq, 3 * GROUP_W), lambda r: (jnp.maximum(r * tbb - 1, 0), 0, 0))]
        operands += [a, a]
    c_specs = [pl.BlockSpec((None,) + c.shape[1:], lambda r: (r, 0, 0, 0)) for c in caches]
    og_spec = pl.BlockSpec((tbb, nq, 2 * GROUP_W), lambda r: (r, 0, 0))
    return pl.pallas_call(
        functools.partial(_attn_kernel, blocks_per_seq=blocks_per_seq),
        grid=(n_req,),
        in_specs=[_resident(qkvt.shape)] + a_specs + c_specs,
        out_specs=[og_spec] * N_GROUPS + c_specs + [pl.BlockSpec((D_ATTN, n_req), lambda r: (0, 0))],
        out_shape=[jax.ShapeDtypeStruct((n_blocks, nq, 2 * GROUP_W), F32)] * N_GROUPS
        + [jax.ShapeDtypeStruct(c.shape, F32) for c in caches]
        + [jax.ShapeDtypeStruct((D_ATTN, n_req), F32)],
        compiler_params=pltpu.CompilerParams(dimension_semantics=("arbitrary",),
                                             vmem_limit_bytes=VMEM_LIMIT_BYTES),
        name="k_attn",
    )(*operands, *caches)


def _merge_groups(o_and_lse):
    m = o_and_lse[0][1]
    for _, lse in o_and_lse[1:]:
        m = jnp.maximum(m, lse)
    es = [jnp.exp(lse - m) for _, lse in o_and_lse]
    inv = 1.0 / (es[0] + es[1] + es[2])
    return jnp.concatenate([(o * (e * inv)).astype(BF16) for (o, _), e in zip(o_and_lse, es)], axis=1)


def _mix_tail(x1, gg, u, mixed, attn, mod_ref, ng_ref, wba_ref, wbb_ref, wo_ref,
              wu_ref, wd_ref, fg_ref, hb_scr, acc_scr):
    gm = (u * mixed).astype(BF16)
    merged = (jax.nn.sigmoid(gg[:, :D_MODEL]) * jnp.dot(attn, wba_ref[...], preferred_element_type=F32)
              + jax.nn.sigmoid(gg[:, D_MODEL:]) * jnp.dot(gm, wbb_ref[...], preferred_element_type=F32))
    y = jnp.dot(merged.astype(BF16), wo_ref[...], preferred_element_type=F32)
    x2 = x1 + _mod(mod_ref, 5) * y
    hb_scr[...] = _pre(x2, ng_ref[2:3, :], mod_ref, 2).astype(BF16)
    f = _swiglu(hb_scr, wu_ref, wd_ref, acc_scr)
    x3 = x2 + (0.5 * _mod(mod_ref, 8)) * f
    return _rms(x3) * fg_ref[...]


def _layer_norm_rows(v, g_ref, b_ref):
    mu = jnp.mean(v, axis=-1, keepdims=True)
    var = jnp.mean(jnp.square(v - mu), axis=-1, keepdims=True)
    return ((v - mu) * lax.rsqrt(var + EPS)) * g_ref[...] + b_ref[...]


def _mix_ffn_prompt_kernel(x1_ref, og0_ref, og1_ref, og2_ref, mod_ref, ng_ref, wr_ref, lng_ref, lnb_ref,
                           ws_ref, bst_ref, wba_ref, wbb_ref, wo_ref, wu_ref, wd_ref, fg_ref,
                           y_ref, vrows_ref, hb_scr, acc_scr, un_scr, mixed_scr):
    tm = x1_ref.shape[0]
    x1 = x1_ref[...]
    h1 = _pre(x1, ng_ref[1:2, :], mod_ref, 1).astype(BF16)
    uv = jnp.dot(h1, wr_ref[:, :2 * D_GMLP], preferred_element_type=F32)
    gg = jnp.dot(h1, wr_ref[:, 2 * D_GMLP:], preferred_element_type=F32)
    u = uv[:, :D_GMLP]
    vn = _layer_norm_rows(uv[:, D_GMLP:], lng_ref, lnb_ref)

    vrows_ref[...] = vn[tm - CHUNK:, :]

    vnb = vn.astype(BF16)
    tri = (lax.broadcasted_iota(jnp.int32, (CHUNK, CHUNK), 0)
           >= lax.broadcasted_iota(jnp.int32, (CHUNK, CHUNK), 1))
    gw = D_GMLP // GMLP_GROUPS
    for g in range(GMLP_GROUPS):
        ws = jnp.where(tri, ws_ref[g], 0.0).astype(BF16)
        bias = bst_ref[:, g:g + 1]
        for c in range(tm // CHUNK):
            mixed_scr[c * CHUNK:(c + 1) * CHUNK, g * gw:(g + 1) * gw] = (
                jnp.dot(ws, vnb[c * CHUNK:(c + 1) * CHUNK, g * gw:(g + 1) * gw],
                        preferred_element_type=F32) + bias)

    for g, og_ref in enumerate((og0_ref, og1_ref, og2_ref)):
        dil = DIL_GROUPS[g][1]
        n = tm // dil
        for r in range(dil):
            for c in range(2 * GROUP_W // LANES):
                un_scr[g, c, pl.ds(r, n, stride=dil), :] = og_ref[r, :, c * LANES:(c + 1) * LANES]
    o_and_lse = []
    half = GROUP_W // LANES
    for g in range(N_GROUPS):
        cols = [un_scr[g, c] for c in range(2 * half)]
        o_and_lse.append((jnp.concatenate(cols[:half], axis=1), jnp.concatenate(cols[half:], axis=1)))
    attn = _merge_groups(o_and_lse)

    y_ref[...] = _mix_tail(x1, gg, u, mixed_scr[...], attn, mod_ref, ng_ref, wba_ref, wbb_ref, wo_ref,
                           wu_ref, wd_ref, fg_ref, hb_scr, acc_scr)


def _mix_ffn_prompt(x1, ogs, mod, norm_g, wr, lng, lnb, ws, bst, wba, wbb, wo, wu, wd, fg):
    bsz, seq_len, _ = x1.shape
    tm = ROW_TILE
    weights = (norm_g, wr, lng, lnb, ws, bst, wba, wbb, wo, wu, wd, fg)
    row_spec = pl.BlockSpec((None, tm, D_MODEL), lambda b, i: (b, i, 0))
    og_specs = [pl.BlockSpec((None, dil, tm // dil, 2 * GROUP_W), lambda b, i: (b, 0, i, 0))
                for _, dil in DIL_GROUPS]
    return pl.pallas_call(
        _mix_ffn_prompt_kernel,
        grid=(bsz, seq_len // tm),
        in_specs=[row_spec] + og_specs + [pl.BlockSpec((None,) + mod.shape[1:], lambda b, i: (b, 0, 0))]
        + [_resident(w.shape) for w in weights],
        out_specs=[row_spec, pl.BlockSpec((None, CHUNK, D_GMLP), lambda b, i: (b, 0, 0))],
        out_shape=[jax.ShapeDtypeStruct(x1.shape, F32), jax.ShapeDtypeStruct((bsz, CHUNK, D_GMLP), F32)],
        scratch_shapes=[pltpu.VMEM((tm, D_MODEL), BF16), pltpu.VMEM((tm, D_MODEL), F32),
                        pltpu.VMEM((N_GROUPS, 2 * GROUP_W // LANES, tm, LANES), F32),
                        pltpu.VMEM((tm, D_GMLP), F32)],
        compiler_params=pltpu.CompilerParams(dimension_semantics=("arbitrary", "arbitrary"),
                                             vmem_limit_bytes=VMEM_LIMIT_BYTES),
        name="k_mix_ffn",
    )(x1, *ogs, mod, *weights)


def _mix_ffn_sample_kernel(x1_ref, attnt_ref, mod_ref, ng_ref, wr_ref, lng_ref, lnb_ref,
                           ws0_ref, bs0_ref, wba_ref, wbb_ref, wo_ref, wu_ref, wd_ref, fg_ref,
                           y_ref, vrows_ref, hb_scr, acc_scr):
    x1 = x1_ref[...]
    h1 = _pre(x1, ng_ref[1:2, :], mod_ref, 1).astype(BF16)
    uv = jnp.dot(h1, wr_ref[:, :2 * D_GMLP], preferred_element_type=F32)
    gg = jnp.dot(h1, wr_ref[:, 2 * D_GMLP:], preferred_element_type=F32)
    u = uv[:, :D_GMLP]
    vn = _layer_norm_rows(uv[:, D_GMLP:], lng_ref, lnb_ref)
    vrows_ref[...] = vn
    mixed = ws0_ref[...] * vn + bs0_ref[...]
    attn = attnt_ref[...].T.astype(BF16)
    y_ref[...] = _mix_tail(x1, gg, u, mixed, attn, mod_ref, ng_ref, wba_ref, wbb_ref, wo_ref,
                           wu_ref, wd_ref, fg_ref, hb_scr, acc_scr)


def _mix_ffn_sample(x1, attnt, mod, norm_g, wr, lng, lnb, ws0, bs0, wba, wbb, wo, wu, wd, fg):
    n = x1.shape[0]
    operands = (x1, attnt, mod, norm_g, wr, lng, lnb, ws0, bs0, wba, wbb, wo, wu, wd, fg)
    return pl.pallas_call(
        _mix_ffn_sample_kernel,
        grid=(1,),
        in_specs=[_resident(w.shape) for w in operands],
        out_specs=[_whole(x1.shape), _whole((n, D_GMLP))],
        out_shape=[jax.ShapeDtypeStruct(x1.shape, F32), jax.ShapeDtypeStruct((n, D_GMLP), F32)],
        scratch_shapes=[pltpu.VMEM((n, D_MODEL), BF16), pltpu.VMEM((n, D_MODEL), F32)],
        compiler_params=pltpu.CompilerParams(dimension_semantics=("arbitrary",),
                                             vmem_limit_bytes=VMEM_LIMIT_BYTES),
        name="k_mix_ffn_sample",
    )(*operands)


def _to_buffer_layout(cache):
    n, rows = cache.shape[0], cache.shape[1]
    return cache.transpose(0, 2, 3, 4, 1).reshape(n, 2 * HEADS_PER_GROUP, HEAD_DIM, rows)


def _from_buffer_layout(buf):
    n, rows = buf.shape[0], buf.shape[-1]
    return buf.reshape(n, 2, HEADS_PER_GROUP, HEAD_DIM, rows).transpose(0, 4, 1, 2, 3)[None]


def kernel(x_prompt, x_sample, c_prompt, c_sample, cache_kv_g0, cache_kv_g1, cache_kv_g2, ada_w, ada_b, norm_g,
           ffn1_up, ffn1_down, w_in, w_branch_a, w_branch_b, w_out, v_ln_g, v_ln_b, w_spatial, b_spatial,
           ffn2_up, ffn2_down, final_g):
    assert ada_w.shape[0] == 1, "one layer"
    bsz, seq_len, _ = x_prompt.shape
    n_req = x_sample.shape[0]

    wu1, wd1 = ffn1_up[0].astype(BF16), ffn1_down[0].astype(BF16)
    wu2, wd2 = ffn2_up[0].astype(BF16), ffn2_down[0].astype(BF16)
    wqkv = w_in[0][:, :3 * D_ATTN].astype(BF16)
    wr = w_in[0][:, 3 * D_ATTN:].astype(BF16)
    wkvt = wqkv[:, D_ATTN:].T
    wba, wbb, wo = w_branch_a[0].astype(BF16), w_branch_b[0].astype(BF16), w_out[0].astype(BF16)
    ng = norm_g[0]
    lng, lnb = v_ln_g, v_ln_b
    ws = w_spatial[0]
    bst = b_spatial[0].T
    gw = D_GMLP // GMLP_GROUPS
    ws0 = jnp.repeat(ws[:, 0, 0], gw)[None, :]
    bs0 = jnp.repeat(b_spatial[0][:, 0], gw)[None, :]
    fg = final_g[None, :]

    mods = _ada_table(jnp.concatenate([c_prompt, c_sample], axis=0), ada_w[0], ada_b)
    mod_p = mods[:, :bsz].transpose(1, 0, 2)
    mod_s = mods[:, bsz:]

    x1s, qkvt = _ffn_qkv_sample(x_sample.reshape(n_req, D_MODEL), mod_s, ng, wu1, wd1, wqkv)
    x1, a0, a1, a2, kv0, kv1, kv2 = _ffn_qkv_prompt(x_prompt, mod_p, ng, wu1, wd1, wqkv, wkvt)

    nq = KEYS_PER_QUERY_BLOCK
    a_blocks = [a.reshape(-1, nq, 3 * GROUP_W) for a in (a0, a1, a2)]
    caches = [_to_buffer_layout(c[0]) for c in (cache_kv_g0, cache_kv_g1, cache_kv_g2)]
    blocks_per_seq = tuple(seq_len // dil // nq for _, dil in DIL_GROUPS)
    og0, og1, og2, s0, s1, s2, attnt = _attention(qkvt, a_blocks, caches, blocks_per_seq)
    ogs = [og.reshape(bsz, dil, seq_len // dil, 2 * GROUP_W) for og, (_, dil) in zip((og0, og1, og2), DIL_GROUPS)]

    y_prompt, vrows_p = _mix_ffn_prompt(x1, ogs, mod_p, ng, wr, lng, lnb, ws, bst, wba, wbb, wo,
                                        wu2, wd2, fg)
    y_sample, vrows_s = _mix_ffn_sample(x1s, attnt, mod_s, ng, wr, lng, lnb, ws0, bs0, wba, wbb, wo,
                                        wu2, wd2, fg)

    return (y_prompt, y_sample.reshape(n_req, 1, D_MODEL),
            _from_buffer_layout(kv0), _from_buffer_layout(kv1), _from_buffer_layout(kv2), vrows_p[None],
            _from_buffer_layout(s0), _from_buffer_layout(s1), _from_buffer_layout(s2),
            vrows_s.reshape(1, n_req, 1, D_GMLP))
```

```python
import functools

import jax
import jax.numpy as jnp
from jax import lax
from jax.experimental import pallas as pl
from jax.experimental.pallas import tpu as pltpu

F32 = jnp.float32
BF16 = jnp.bfloat16

D_MODEL = 1024
HEAD_DIM = 64
HEADS_PER_GROUP = 4
GROUP_W = HEADS_PER_GROUP * HEAD_DIM
DIL_GROUPS = ((128, 1), (512, 4), (2048, 16))
N_GROUPS = len(DIL_GROUPS)
D_ATTN = N_GROUPS * GROUP_W
D_GMLP = 512
GMLP_GROUPS = 4
CHUNK = 128
D_FF = 2816
FF_CHUNK = 256
N_FF_CHUNKS = D_FF // FF_CHUNK
EPS = 1e-6
QK_SCALE = HEAD_DIM ** -0.5
KEYS_PER_QUERY_BLOCK = 128
MASKED = -1e30
LANES = 128

VMEM_LIMIT_BYTES = 60 * 1024 * 1024
ROW_TILE = 512


def _resident(shape):
    zeros = (0,) * len(shape)
    return pl.BlockSpec(shape, lambda *_: zeros, pipeline_mode=pl.Buffered(1))


def _whole(shape):
    zeros = (0,) * len(shape)
    return pl.BlockSpec(shape, lambda *_: zeros)


def _rms(x):
    return x * lax.rsqrt(jnp.mean(x * x, axis=-1, keepdims=True) + EPS)


def _mod(mod_ref, i):
    if len(mod_ref.shape) == 2:
        return mod_ref[i:i + 1, :]
    return mod_ref[i]


def _pre(x, g, mod_ref, sub):
    return (_rms(x) * g) * (1.0 + _mod(mod_ref, 3 * sub + 1)) + _mod(mod_ref, 3 * sub)


def _swiglu_chunk(c, h_ref, wu_ref, wd_ref, acc_ref):
    lo, hi = c * FF_CHUNK, (c + 1) * FF_CHUNK
    h = h_ref[...]
    a = jnp.dot(h, wu_ref[:, lo:hi], preferred_element_type=F32)
    b = jnp.dot(h, wu_ref[:, D_FF + lo:D_FF + hi], preferred_element_type=F32)
    act = ((a * jax.nn.sigmoid(a)) * b).astype(BF16)
    part = jnp.dot(act, wd_ref[lo:hi, :], preferred_element_type=F32)
    if c == 0:
        acc_ref[...] = part
    else:
        acc_ref[...] += part


def _swiglu(h_ref, wu_ref, wd_ref, acc_ref, side_tasks=()):
    dealt = 0
    for c in range(N_FF_CHUNKS):
        _swiglu_chunk(c, h_ref, wu_ref, wd_ref, acc_ref)
        while dealt * N_FF_CHUNKS < len(side_tasks) * (c + 1):
            side_tasks[dealt]()
            dealt += 1
    return acc_ref[...]


def _ada_kernel(c_ref, w_ref, b_ref, o_ref):
    c = c_ref[...]
    h = (c * jax.nn.sigmoid(c)).astype(BF16)
    o_ref[...] = jnp.dot(h, w_ref[...].astype(BF16), preferred_element_type=F32) + b_ref[...]


def _ada_table(c_all, ada_w, ada_b):
    n = c_all.shape[0]
    n_ada = ada_w.shape[1] // D_MODEL
    return pl.pallas_call(
        _ada_kernel,
        grid=(n_ada,),
        in_specs=[
            pl.BlockSpec((n, D_MODEL), lambda i: (0, 0)),
            pl.BlockSpec((D_MODEL, D_MODEL), lambda i: (0, i)),
            pl.BlockSpec((1, D_MODEL), lambda i: (0, i)),
        ],
        out_specs=pl.BlockSpec((None, n, D_MODEL), lambda i: (i, 0, 0)),
        out_shape=jax.ShapeDtypeStruct((n_ada, n, D_MODEL), F32),
        compiler_params=pltpu.CompilerParams(dimension_semantics=("arbitrary",)),
        name="k_ada",
    )(c_all, ada_w, ada_b)


def _ffn1_front(x_ref, mod_ref, ng_ref, wu_ref, wd_ref, hb_scr, acc_scr, side_tasks=()):
    x = x_ref[...]
    hb_scr[...] = _pre(x, ng_ref[0:1, :], mod_ref, 0).astype(BF16)
    f = _swiglu(hb_scr, wu_ref, wd_ref, acc_scr, side_tasks)
    x1 = x + (0.5 * _mod(mod_ref, 2)) * f
    return x1, _pre(x1, ng_ref[1:2, :], mod_ref, 1)


def _ffn_qkv_prompt_kernel(x_ref, mod_ref, ng_ref, wu_ref, wd_ref, wqkv_ref, wkvt_ref,
                           qkvt_ref, ck0_ref, ck1_ref, ck2_ref,
                           x1_ref, a0_ref, a1_ref, a2_ref, kv0_ref, kv1_ref, kv2_ref,
                           nk0_ref, nk1_ref, nk2_ref, pn0_ref, pn1_ref, pn2_ref, stats_ref,
                           hb_scr, acc_scr, h1_scr, hp_scr, *, seq_len):
    tm = x_ref.shape[0]
    i = pl.program_id(1)

    request = pl.program_id(0) * pl.num_programs(1) + i
    onehot = (lax.broadcasted_iota(jnp.int32, (1, qkvt_ref.shape[1]), 1) == request).astype(F32)
    cols = jnp.sum(qkvt_ref[0:2 * D_ATTN, :] * onehot, axis=1, keepdims=True)

    def scores_task(g, h, c_ref, n_ref, pn_ref):
        p, p_new, l, lse = _sample_scores(g, h, cols, c_ref, n_ref)
        inv_l = 1.0 / l
        pn_ref[h:h + 1, :] = p * inv_l
        row = g * HEADS_PER_GROUP + h
        stats_ref[row:row + 1, :] = jnp.broadcast_to(p_new * inv_l, (1, LANES))
        n_heads = N_GROUPS * HEADS_PER_GROUP
        stats_ref[n_heads + row:n_heads + row + 1, :] = jnp.broadcast_to(lse, (1, LANES))

    side_tasks = [functools.partial(scores_task, g, h, c_ref, n_ref, pn_ref)
                  for g, (c_ref, n_ref, pn_ref) in enumerate(((ck0_ref, nk0_ref, pn0_ref), (ck1_ref, nk1_ref, pn1_ref),
                                                              (ck2_ref, nk2_ref, pn2_ref)))
                  for h in range(HEADS_PER_GROUP)]
    x1, h1 = _ffn1_front(x_ref, mod_ref, ng_ref, wu_ref, wd_ref, hb_scr, acc_scr, side_tasks)
    x1_ref[...] = x1
    for c in range(D_MODEL // LANES):
        h1_scr[c] = h1[:, c * LANES:(c + 1) * LANES]
    hb_scr[...] = h1.astype(BF16)

    for g, a_ref in enumerate((a0_ref, a1_ref, a2_ref)):
        dil = DIL_GROUPS[g][1]
        n = tm // dil
        if dil == 1:
            hp = hb_scr[...]
        else:
            for r in range(dil):
                rows = [h1_scr[c, pl.ds(r, n, stride=dil), :] for c in range(D_MODEL // LANES)]
                hp_scr[r * n:(r + 1) * n, :] = jnp.concatenate(rows, axis=1).astype(BF16)
            hp = hp_scr[...]
        for part in range(3):
            lo = part * D_ATTN + g * GROUP_W
            res = jnp.dot(hp, wqkv_ref[:, lo:lo + GROUP_W], preferred_element_type=F32)
            if part == 0:
                res = res * QK_SCALE
            a_ref[:, :, part * GROUP_W:(part + 1) * GROUP_W] = res.astype(BF16).reshape(dil, n, GROUP_W)

    for g, kv_ref in enumerate((kv0_ref, kv1_ref, kv2_ref)):
        window = DIL_GROUPS[g][0]
        first_tile = (seq_len - window) // tm

        @pl.when(i >= first_tile)
        def _(g=g, kv_ref=kv_ref, window=window):
            for part in range(2):
                lo = part * D_ATTN + g * GROUP_W
                t = lax.dot_general(wkvt_ref[lo:lo + GROUP_W, :], hb_scr[...], (((1,), (1,)), ((), ())),
                                    preferred_element_type=F32)
                if window < tm:
                    t = t[:, tm - window:]
                kv_ref[part * HEADS_PER_GROUP:(part + 1) * HEADS_PER_GROUP] = t.reshape(
                    HEADS_PER_GROUP, HEAD_DIM, t.shape[-1])


def _ffn_qkv_prompt(x, mod, norm_g, wu, wd, wqkv, wkvt, qkvt, caches):
    bsz, seq_len, _ = x.shape
    tm = ROW_TILE
    nt = seq_len // tm
    n_req = qkvt.shape[1]
    assert bsz * nt == n_req, "one sample request rides on each row-tile grid step"
    weights = (norm_g, wu, wd, wqkv, wkvt, qkvt)
    n_heads = N_GROUPS * HEADS_PER_GROUP
    k_half = lambda c: pl.BlockSpec((None, HEADS_PER_GROUP) + c.shape[2:], lambda b, i: (b * nt + i, 0, 0, 0))
    side_in_specs = [k_half(c) for c in caches]
    side_out_specs = ([k_half(c) for c in caches]
                      + [pl.BlockSpec((None, HEADS_PER_GROUP, c.shape[-1]), lambda b, i: (b * nt + i, 0, 0))
                         for c in caches]
                      + [pl.BlockSpec((None, 2 * n_heads, LANES), lambda b, i: (b * nt + i, 0, 0))])
    side_out_shapes = ([jax.ShapeDtypeStruct(c.shape, F32) for c in caches]
                       + [jax.ShapeDtypeStruct((n_req, HEADS_PER_GROUP, c.shape[-1]), F32) for c in caches]
                       + [jax.ShapeDtypeStruct((n_req, 2 * n_heads, LANES), F32)])
    a_shapes, a_specs, kv_shapes, kv_specs = [], [], [], []
    for window, dil in DIL_GROUPS:
        a_shapes.append(jax.ShapeDtypeStruct((bsz, dil, seq_len // dil, 3 * GROUP_W), BF16))
        a_specs.append(pl.BlockSpec((None, dil, tm // dil, 3 * GROUP_W), lambda b, i: (b, 0, i, 0)))
        first_tile = (seq_len - window) // tm
        kv_shapes.append(jax.ShapeDtypeStruct((bsz, 2 * HEADS_PER_GROUP, HEAD_DIM, window), F32))
        kv_specs.append(pl.BlockSpec(
            (None, 2 * HEADS_PER_GROUP, HEAD_DIM, min(window, tm)),
            lambda b, i, first_tile=first_tile: (b, 0, 0, jnp.maximum(i - first_tile, 0))))
    row_spec = pl.BlockSpec((None, tm, D_MODEL), lambda b, i: (b, i, 0))
    return pl.pallas_call(
        functools.partial(_ffn_qkv_prompt_kernel, seq_len=seq_len),
        grid=(bsz, seq_len // tm),
        in_specs=[row_spec, pl.BlockSpec((None,) + mod.shape[1:], lambda b, i: (b, 0, 0))]
        + [_resident(w.shape) for w in weights] + side_in_specs,
        out_specs=[row_spec] + a_specs + kv_specs + side_out_specs,
        out_shape=[jax.ShapeDtypeStruct(x.shape, F32)] + a_shapes + kv_shapes + side_out_shapes,
        scratch_shapes=[pltpu.VMEM((tm, D_MODEL), BF16), pltpu.VMEM((tm, D_MODEL), F32),
                        pltpu.VMEM((D_MODEL // LANES, tm, LANES), F32), pltpu.VMEM((tm, D_MODEL), BF16)],
        compiler_params=pltpu.CompilerParams(dimension_semantics=("arbitrary", "arbitrary"),
                                             vmem_limit_bytes=VMEM_LIMIT_BYTES),
        name="k_ffn_qkv",
    )(x, mod, *weights, *caches)


def _ffn_qkv_sample_kernel(x_ref, mod_ref, ng_ref, wu_ref, wd_ref, wqkv_ref,
                           x1_ref, qkvt_ref, hb_scr, acc_scr):
    x1, h1 = _ffn1_front(x_ref, mod_ref, ng_ref, wu_ref, wd_ref, hb_scr, acc_scr)
    x1_ref[...] = x1
    qkv = jnp.dot(h1.astype(BF16), wqkv_ref[...], preferred_element_type=F32)
    col = lax.broadcasted_iota(jnp.int32, (1, 3 * D_ATTN), 1)
    qkv = qkv * jnp.where(col < D_ATTN, QK_SCALE, 1.0)
    qkvt_ref[...] = qkv.T


def _ffn_qkv_sample(x, mod, norm_g, wu, wd, wqkv):
    n = x.shape[0]
    weights = (norm_g, wu, wd, wqkv)
    return pl.pallas_call(
        _ffn_qkv_sample_kernel,
        grid=(1,),
        in_specs=[_resident(x.shape), _resident(mod.shape)] + [_resident(w.shape) for w in weights],
        out_specs=[_whole(x.shape), _whole((3 * D_ATTN, n))],
        out_shape=[jax.ShapeDtypeStruct(x.shape, F32), jax.ShapeDtypeStruct((3 * D_ATTN, n), F32)],
        scratch_shapes=[pltpu.VMEM((n, D_MODEL), BF16), pltpu.VMEM((n, D_MODEL), F32)],
        compiler_params=pltpu.CompilerParams(dimension_semantics=("arbitrary",),
                                             vmem_limit_bytes=VMEM_LIMIT_BYTES),
        name="k_ffn_qkv_sample",
    )(x, mod, *weights)


def _band_scores(q, k2, lane_head):
    qs = jnp.concatenate([jnp.where(lane_head == h, q, jnp.zeros_like(q)) for h in range(HEADS_PER_GROUP)], axis=0)
    return lax.dot_general(qs, k2, (((1,), (1,)), ((), ())), preferred_element_type=F32)


def _band_softmax_values(s, v2, first_key, band_key, lane_head, ones):
    nq = KEYS_PER_QUERY_BLOCK
    s = jnp.where(band_key >= first_key, s, MASKED)
    m = jnp.max(s, axis=1, keepdims=True)
    p = jnp.exp(s - m).astype(BF16)
    r = jnp.dot(p, v2, preferred_element_type=F32)
    l = jnp.dot(p, ones, preferred_element_type=F32)
    inv_l = 1.0 / l
    lse = m + jnp.log(l)
    o = lse_b = None
    for h in range(HEADS_PER_GROUP):
        rows = slice(h * nq, (h + 1) * nq)
        o_h = r[rows] * jnp.concatenate([inv_l[rows]] * (GROUP_W // LANES), axis=1)
        lse_h = jnp.concatenate([lse[rows]] * (GROUP_W // LANES), axis=1)
        o = o_h if h == 0 else jnp.where(lane_head == h, o_h, o)
        lse_b = lse_h if h == 0 else jnp.where(lane_head == h, lse_h, lse_b)
    return o, lse_b


def _shifted(buf, newest, new_col):
    return jnp.where(newest, new_col, pltpu.roll(buf, buf.shape[-1] - 1, axis=1))


def _sample_scores(g, h, cols, c_ref, n_ref):
    window, dil = DIL_GROUPS[g]
    rows = c_ref.shape[-1]
    col = lax.broadcasted_iota(jnp.int32, (1, rows), 1)
    dist = rows - col
    strided = (dist % dil == 0) & (dist <= window)
    base = g * GROUP_W + h * HEAD_DIM
    qc = cols[base:base + HEAD_DIM]
    kc = cols[D_ATTN + base:D_ATTN + base + HEAD_DIM]
    kt = c_ref[h]
    s = jnp.where(strided, jnp.sum(kt * qc, axis=0, keepdims=True), MASKED)
    s_new = jnp.sum(kc * qc, axis=0, keepdims=True)
    m = jnp.maximum(jnp.max(s, axis=1, keepdims=True), s_new)
    p = jnp.exp(s - m)
    p_new = jnp.exp(s_new - m)
    l = jnp.sum(p, axis=1, keepdims=True) + p_new
    n_ref[h] = _shifted(kt, col == rows - 1, kc)
    return p, p_new, l, m + jnp.log(l)


def _sample_values(g, h, v_cols, c_ref, n_ref, pn, pn_new):
    rows = c_ref.shape[-1]
    col = lax.broadcasted_iota(jnp.int32, (1, rows), 1)
    base = g * GROUP_W + h * HEAD_DIM
    vc = v_cols[base:base + HEAD_DIM]
    vt = c_ref[h]
    n_ref[h] = _shifted(vt, col == rows - 1, vc)
    return jnp.sum(vt * pn, axis=1, keepdims=True) + vc * pn_new


def _attn_kernel(qkvt_ref, a0_ref, p0_ref, a1_ref, p1_ref, a2_ref, p2_ref, c0_ref, c1_ref, c2_ref,
                 pn0_ref, pn1_ref, pn2_ref, stats_ref, nk0_ref, nk1_ref, nk2_ref,
                 og0_ref, og1_ref, og2_ref, n0_ref, n1_ref, n2_ref, attnt_ref, *, blocks_per_seq):
    del nk0_ref, nk1_ref, nk2_ref
    step = pl.program_id(0)
    nq = KEYS_PER_QUERY_BLOCK

    @pl.when(step == 0)
    def _():
        attnt_ref[...] = jnp.zeros_like(attnt_ref)

    lane_head = lax.broadcasted_iota(jnp.int32, (nq, GROUP_W), 1) // HEAD_DIM
    qi = lax.broadcasted_iota(jnp.int32, (HEADS_PER_GROUP * nq, 2 * nq), 0) % nq
    kj = lax.broadcasted_iota(jnp.int32, (HEADS_PER_GROUP * nq, 2 * nq), 1)
    band_key = jnp.where((kj >= qi) & (kj <= qi + nq), kj, -1)
    ones = jnp.ones((2 * nq, LANES), BF16)

    scores = {}

    def scores_of(g, t, a_ref, p_ref, og_ref):
        before = p_ref[0] if t == 0 else a_ref[t - 1]
        own = a_ref[t]
        k2 = jnp.concatenate([before[:, GROUP_W:2 * GROUP_W], own[:, GROUP_W:2 * GROUP_W]], axis=0)
        scores[g, t] = _band_scores(own[:, 0:GROUP_W], k2, lane_head)

    def band_task(g, t, a_ref, p_ref, og_ref):
        tbb = a_ref.shape[0]
        before = p_ref[0] if t == 0 else a_ref[t - 1]
        v2 = jnp.concatenate([before[:, 2 * GROUP_W:], a_ref[t][:, 2 * GROUP_W:]], axis=0)
        first_key = jnp.where(((step * tbb + t) % blocks_per_seq[g]) == 0, nq, 0)
        o, lse_b = _band_softmax_values(scores.pop((g, t)), v2, first_key, band_key, lane_head, ones)
        og_ref[t, :, 0:GROUP_W] = o
        og_ref[t, :, GROUP_W:] = lse_b

    blocks = [(g, t, a_ref, p_ref, og_ref)
              for g, (a_ref, p_ref, og_ref) in enumerate(((a0_ref, p0_ref, og0_ref), (a1_ref, p1_ref, og1_ref),
                                                          (a2_ref, p2_ref, og2_ref)))
              for t in range(a_ref.shape[0])]
    score_ahead = 2
    for blk in blocks[:score_ahead]:
        scores_of(*blk)

    def make_band_task(i):
        def task():
            band_task(*blocks[i])
            if i + score_ahead < len(blocks):
                scores_of(*blocks[i + score_ahead])
        return task

    band_tasks = [make_band_task(i) for i in range(len(blocks))]

    n_req = qkvt_ref.shape[1]
    onehot = (lax.broadcasted_iota(jnp.int32, (1, n_req), 1) == step).astype(F32)
    v_cols = jnp.sum(qkvt_ref[2 * D_ATTN:, :] * onehot, axis=1, keepdims=True)
    n_heads = N_GROUPS * HEADS_PER_GROUP
    heads = {}

    def values_task(g, h, c_ref, n_ref, pn_ref):
        row = g * HEADS_PER_GROUP + h
        pn_new = stats_ref[row:row + 1, 0:1]
        lse = stats_ref[n_heads + row:n_heads + row + 1, 0:1]
        heads[g, h] = (_sample_values(g, h, v_cols, c_ref, n_ref, pn_ref[h:h + 1, :], pn_new), lse)

    sample_tasks = []
    for g, (c_ref, n_ref, pn_ref) in enumerate(((c0_ref, n0_ref, pn0_ref), (c1_ref, n1_ref, pn1_ref),
                                                (c2_ref, n2_ref, pn2_ref))):
        for h in range(HEADS_PER_GROUP):
            sample_tasks.append((c_ref.shape[-1], functools.partial(values_task, g, h, c_ref, n_ref, pn_ref)))

    total = sum(cost for cost, _ in sample_tasks)
    dealt = nxt = 0
    for i, task in enumerate(band_tasks):
        task()
        while nxt < len(sample_tasks) and dealt * len(band_tasks) < total * (i + 1):
            cost, sample_task = sample_tasks[nxt]
            sample_task()
            dealt += cost
            nxt += 1
    for _, sample_task in sample_tasks[nxt:]:
        sample_task()

    for h in range(HEADS_PER_GROUP):
        lses = [heads[g, h][1] for g in range(N_GROUPS)]
        m = jnp.maximum(jnp.maximum(lses[0], lses[1]), lses[2])
        es = [jnp.exp(lse - m) for lse in lses]
        inv = 1.0 / (es[0] + es[1] + es[2])
        for g in range(N_GROUPS):
            base = g * GROUP_W + h * HEAD_DIM
            attnt_ref[base:base + HEAD_DIM, :] += (heads[g, h][0] * (es[g] * inv)) * onehot


def _attention(qkvt, a_blocks, caches, half_shifted, weights, stats, blocks_per_seq):
    n_req = qkvt.shape[1]
    n_blocks = a_blocks[0].shape[0]
    assert n_blocks % n_req == 0, "query blocks are dealt evenly over the per-request grid steps"
    tbb = n_blocks // n_req
    nq = KEYS_PER_QUERY_BLOCK
    a_specs, operands = [], [qkvt]
    for a in a_blocks:
        a_specs += [pl.BlockSpec((tbb, nq, 3 * GROUP_W), lambda r: (r, 0, 0)),
                    pl.BlockSpec((1, nq, 3 * GROUP_W), lambda r: (jnp.maximum(r * tbb - 1, 0), 0, 0))]
        operands += [a, a]
    v_specs = [pl.BlockSpec((None, HEADS_PER_GROUP) + c.shape[2:], lambda r: (r, 1, 0, 0)) for c in caches]
    w_specs = [pl.BlockSpec((None,) + w.shape[1:], lambda r: (r, 0, 0)) for w in weights]
    stats_spec = pl.BlockSpec((None,) + stats.shape[1:], lambda r: (r, 0, 0))
    og_spec = pl.BlockSpec((tbb, nq, 2 * GROUP_W), lambda r: (r, 0, 0))
    operands += [*caches, *weights, stats, *half_shifted]
    first_aliased = len(operands) - len(half_shifted)
    return pl.pallas_call(
        functools.partial(_attn_kernel, blocks_per_seq=blocks_per_seq),
        grid=(n_req,),
        in_specs=[_resident(qkvt.shape)] + a_specs + v_specs + w_specs + [stats_spec]
        + [pl.BlockSpec(memory_space=pl.ANY)] * len(half_shifted),
        out_specs=[og_spec] * N_GROUPS + v_specs + [pl.BlockSpec((D_ATTN, n_req), lambda r: (0, 0))],
        out_shape=[jax.ShapeDtypeStruct((n_blocks, nq, 2 * GROUP_W), F32)] * N_GROUPS
        + [jax.ShapeDtypeStruct(c.shape, F32) for c in caches]
        + [jax.ShapeDtypeStruct((D_ATTN, n_req), F32)],
        input_output_aliases={first_aliased + g: N_GROUPS + g for g in range(len(half_shifted))},
        compiler_params=pltpu.CompilerParams(dimension_semantics=("arbitrary",),
                                             vmem_limit_bytes=VMEM_LIMIT_BYTES),
        name="k_attn",
    )(*operands)


def _merge_groups(o_and_lse):
    m = o_and_lse[0][1]
    for _, lse in o_and_lse[1:]:
        m = jnp.maximum(m, lse)
    es = [jnp.exp(lse - m) for _, lse in o_and_lse]
    inv = 1.0 / (es[0] + es[1] + es[2])
    return jnp.concatenate([(o * (e * inv)).astype(BF16) for (o, _), e in zip(o_and_lse, es)], axis=1)


def _mix_tail(x1, gg, u, mixed, attn, mod_ref, ng_ref, wba_ref, wbb_ref, wo_ref,
              wu_ref, wd_ref, fg_ref, hb_scr, acc_scr):
    gm = (u * mixed).astype(BF16)
    merged = (jax.nn.sigmoid(gg[:, :D_MODEL]) * jnp.dot(attn, wba_ref[...], preferred_element_type=F32)
              + jax.nn.sigmoid(gg[:, D_MODEL:]) * jnp.dot(gm, wbb_ref[...], preferred_element_type=F32))
    y = jnp.dot(merged.astype(BF16), wo_ref[...], preferred_element_type=F32)
    x2 = x1 + _mod(mod_ref, 5) * y
    hb_scr[...] = _pre(x2, ng_ref[2:3, :], mod_ref, 2).astype(BF16)
    f = _swiglu(hb_scr, wu_ref, wd_ref, acc_scr)
    x3 = x2 + (0.5 * _mod(mod_ref, 8)) * f
    return _rms(x3) * fg_ref[...]


def _layer_norm_rows(v, g_ref, b_ref):
    mu = jnp.mean(v, axis=-1, keepdims=True)
    var = jnp.mean(jnp.square(v - mu), axis=-1, keepdims=True)
    return ((v - mu) * lax.rsqrt(var + EPS)) * g_ref[...] + b_ref[...]


def _mix_ffn_prompt_kernel(x1_ref, og0_ref, og1_ref, og2_ref, mod_ref, ng_ref, wr_ref, lng_ref, lnb_ref,
                           ws_ref, bst_ref, wba_ref, wbb_ref, wo_ref, wu_ref, wd_ref, fg_ref,
                           y_ref, vrows_ref, hb_scr, acc_scr, un_scr, mixed_scr):
    tm = x1_ref.shape[0]
    x1 = x1_ref[...]
    h1 = _pre(x1, ng_ref[1:2, :], mod_ref, 1).astype(BF16)
    uv = jnp.dot(h1, wr_ref[:, :2 * D_GMLP], preferred_element_type=F32)
    gg = jnp.dot(h1, wr_ref[:, 2 * D_GMLP:], preferred_element_type=F32)
    u = uv[:, :D_GMLP]
    vn = _layer_norm_rows(uv[:, D_GMLP:], lng_ref, lnb_ref)

    vrows_ref[...] = vn[tm - CHUNK:, :]

    vnb = vn.astype(BF16)
    tri = (lax.broadcasted_iota(jnp.int32, (CHUNK, CHUNK), 0)
           >= lax.broadcasted_iota(jnp.int32, (CHUNK, CHUNK), 1))
    gw = D_GMLP // GMLP_GROUPS
    for g in range(GMLP_GROUPS):
        ws = jnp.where(tri, ws_ref[g], 0.0).astype(BF16)
        bias = bst_ref[:, g:g + 1]
        for c in range(tm // CHUNK):
            mixed_scr[c * CHUNK:(c + 1) * CHUNK, g * gw:(g + 1) * gw] = (
                jnp.dot(ws, vnb[c * CHUNK:(c + 1) * CHUNK, g * gw:(g + 1) * gw],
                        preferred_element_type=F32) + bias)

    for g, og_ref in enumerate((og0_ref, og1_ref, og2_ref)):
        dil = DIL_GROUPS[g][1]
        n = tm // dil
        for r in range(dil):
            for c in range(2 * GROUP_W // LANES):
                un_scr[g, c, pl.ds(r, n, stride=dil), :] = og_ref[r, :, c * LANES:(c + 1) * LANES]
    o_and_lse = []
    half = GROUP_W // LANES
    for g in range(N_GROUPS):
        cols = [un_scr[g, c] for c in range(2 * half)]
        o_and_lse.append((jnp.concatenate(cols[:half], axis=1), jnp.concatenate(cols[half:], axis=1)))
    attn = _merge_groups(o_and_lse)

    y_ref[...] = _mix_tail(x1, gg, u, mixed_scr[...], attn, mod_ref, ng_ref, wba_ref, wbb_ref, wo_ref,
                           wu_ref, wd_ref, fg_ref, hb_scr, acc_scr)


def _mix_ffn_prompt(x1, ogs, mod, norm_g, wr, lng, lnb, ws, bst, wba, wbb, wo, wu, wd, fg):
    bsz, seq_len, _ = x1.shape
    tm = ROW_TILE
    weights = (norm_g, wr, lng, lnb, ws, bst, wba, wbb, wo, wu, wd, fg)
    row_spec = pl.BlockSpec((None, tm, D_MODEL), lambda b, i: (b, i, 0))
    og_specs = [pl.BlockSpec((None, dil, tm // dil, 2 * GROUP_W), lambda b, i: (b, 0, i, 0))
                for _, dil in DIL_GROUPS]
    return pl.pallas_call(
        _mix_ffn_prompt_kernel,
        grid=(bsz, seq_len // tm),
        in_specs=[row_spec] + og_specs + [pl.BlockSpec((None,) + mod.shape[1:], lambda b, i: (b, 0, 0))]
        + [_resident(w.shape) for w in weights],
        out_specs=[row_spec, pl.BlockSpec((None, CHUNK, D_GMLP), lambda b, i: (b, 0, 0))],
        out_shape=[jax.ShapeDtypeStruct(x1.shape, F32), jax.ShapeDtypeStruct((bsz, CHUNK, D_GMLP), F32)],
        scratch_shapes=[pltpu.VMEM((tm, D_MODEL), BF16), pltpu.VMEM((tm, D_MODEL), F32),
                        pltpu.VMEM((N_GROUPS, 2 * GROUP_W // LANES, tm, LANES), F32),
                        pltpu.VMEM((tm, D_GMLP), F32)],
        compiler_params=pltpu.CompilerParams(dimension_semantics=("arbitrary", "arbitrary"),
                                             vmem_limit_bytes=VMEM_LIMIT_BYTES),
        name="k_mix_ffn",
    )(x1, *ogs, mod, *weights)


def _mix_ffn_sample_kernel(x1_ref, attnt_ref, mod_ref, ng_ref, wr_ref, lng_ref, lnb_ref,
                           ws0_ref, bs0_ref, wba_ref, wbb_ref, wo_ref, wu_ref, wd_ref, fg_ref,
                           y_ref, vrows_ref, hb_scr, acc_scr):
    x1 = x1_ref[...]
    h1 = _pre(x1, ng_ref[1:2, :], mod_ref, 1).astype(BF16)
    uv = jnp.dot(h1, wr_ref[:, :2 * D_GMLP], preferred_element_type=F32)
    gg = jnp.dot(h1, wr_ref[:, 2 * D_GMLP:], preferred_element_type=F32)
    u = uv[:, :D_GMLP]
    vn = _layer_norm_rows(uv[:, D_GMLP:], lng_ref, lnb_ref)
    vrows_ref[...] = vn
    mixed = ws0_ref[...] * vn + bs0_ref[...]
    attn = attnt_ref[...].T.astype(BF16)
    y_ref[...] = _mix_tail(x1, gg, u, mixed, attn, mod_ref, ng_ref, wba_ref, wbb_ref, wo_ref,
                           wu_ref, wd_ref, fg_ref, hb_scr, acc_scr)


def _mix_ffn_sample(x1, attnt, mod, norm_g, wr, lng, lnb, ws0, bs0, wba, wbb, wo, wu, wd, fg):
    n = x1.shape[0]
    operands = (x1, attnt, mod, norm_g, wr, lng, lnb, ws0, bs0, wba, wbb, wo, wu, wd, fg)
    return pl.pallas_call(
        _mix_ffn_sample_kernel,
        grid=(1,),
        in_specs=[_resident(w.shape) for w in operands],
        out_specs=[_whole(x1.shape), _whole((n, D_GMLP))],
        out_shape=[jax.ShapeDtypeStruct(x1.shape, F32), jax.ShapeDtypeStruct((n, D_GMLP), F32)],
        scratch_shapes=[pltpu.VMEM((n, D_MODEL), BF16), pltpu.VMEM((n, D_MODEL), F32)],
        compiler_params=pltpu.CompilerParams(dimension_semantics=("arbitrary",),
                                             vmem_limit_bytes=VMEM_LIMIT_BYTES),
        name="k_mix_ffn_sample",
    )(*operands)


def _to_buffer_layout(cache):
    n, rows = cache.shape[0], cache.shape[1]
    return cache.transpose(0, 2, 3, 4, 1).reshape(n, 2 * HEADS_PER_GROUP, HEAD_DIM, rows)


def _from_buffer_layout(buf):
    n, rows = buf.shape[0], buf.shape[-1]
    return buf.reshape(n, 2, HEADS_PER_GROUP, HEAD_DIM, rows).transpose(0, 4, 1, 2, 3)[None]


def kernel(x_prompt, x_sample, c_prompt, c_sample, cache_kv_g0, cache_kv_g1, cache_kv_g2, ada_w, ada_b, norm_g,
           ffn1_up, ffn1_down, w_in, w_branch_a, w_branch_b, w_out, v_ln_g, v_ln_b, w_spatial, b_spatial,
           ffn2_up, ffn2_down, final_g):
    assert ada_w.shape[0] == 1, "one layer"
    bsz, seq_len, _ = x_prompt.shape
    n_req = x_sample.shape[0]

    wu1, wd1 = ffn1_up[0].astype(BF16), ffn1_down[0].astype(BF16)
    wu2, wd2 = ffn2_up[0].astype(BF16), ffn2_down[0].astype(BF16)
    wqkv = w_in[0][:, :3 * D_ATTN].astype(BF16)
    wr = w_in[0][:, 3 * D_ATTN:].astype(BF16)
    wkvt = wqkv[:, D_ATTN:].T
    wba, wbb, wo = w_branch_a[0].astype(BF16), w_branch_b[0].astype(BF16), w_out[0].astype(BF16)
    ng = norm_g[0]
    lng, lnb = v_ln_g, v_ln_b
    ws = w_spatial[0]
    bst = b_spatial[0].T
    gw = D_GMLP // GMLP_GROUPS
    ws0 = jnp.repeat(ws[:, 0, 0], gw)[None, :]
    bs0 = jnp.repeat(b_spatial[0][:, 0], gw)[None, :]
    fg = final_g[None, :]

    mods = _ada_table(jnp.concatenate([c_prompt, c_sample], axis=0), ada_w[0], ada_b)
    mod_p = mods[:, :bsz].transpose(1, 0, 2)
    mod_s = mods[:, bsz:]

    x1s, qkvt = _ffn_qkv_sample(x_sample.reshape(n_req, D_MODEL), mod_s, ng, wu1, wd1, wqkv)
    caches = [_to_buffer_layout(c[0]) for c in (cache_kv_g0, cache_kv_g1, cache_kv_g2)]
    (x1, a0, a1, a2, kv0, kv1, kv2, nk0, nk1, nk2, pn0, pn1, pn2, stats) = _ffn_qkv_prompt(
        x_prompt, mod_p, ng, wu1, wd1, wqkv, wkvt, qkvt, caches)

    nq = KEYS_PER_QUERY_BLOCK
    a_blocks = [a.reshape(-1, nq, 3 * GROUP_W) for a in (a0, a1, a2)]
    blocks_per_seq = tuple(seq_len // dil // nq for _, dil in DIL_GROUPS)
    og0, og1, og2, s0, s1, s2, attnt = _attention(qkvt, a_blocks, caches, (nk0, nk1, nk2), (pn0, pn1, pn2), stats,
                                                  blocks_per_seq)
    ogs = [og.reshape(bsz, dil, seq_len // dil, 2 * GROUP_W) for og, (_, dil) in zip((og0, og1, og2), DIL_GROUPS)]

    y_prompt, vrows_p = _mix_ffn_prompt(x1, ogs, mod_p, ng, wr, lng, lnb, ws, bst, wba, wbb, wo,
                                        wu2, wd2, fg)
    y_sample, vrows_s = _mix_ffn_sample(x1s, attnt, mod_s, ng, wr, lng, lnb, ws0, bs0, wba, wbb, wo,
                                        wu2, wd2, fg)

    return (y_prompt, y_sample.reshape(n_req, 1, D_MODEL),
            _from_buffer_layout(kv0), _from_buffer_layout(kv1), _from_buffer_layout(kv2), vrows_p[None],
            _from_buffer_layout(s0), _from_buffer_layout(s1), _from_buffer_layout(s2),
            vrows_s.reshape(1, n_req, 1, D_GMLP))
```

```python
import functools

import jax
import jax.numpy as jnp
from jax import lax
from jax.experimental import pallas as pl
from jax.experimental.pallas import tpu as pltpu

F32 = jnp.float32
BF16 = jnp.bfloat16

D_MODEL = 1024
HEAD_DIM = 64
HEADS_PER_GROUP = 4
GROUP_W = HEADS_PER_GROUP * HEAD_DIM
DIL_GROUPS = ((128, 1), (512, 4), (2048, 16))
N_GROUPS = len(DIL_GROUPS)
D_ATTN = N_GROUPS * GROUP_W
D_GMLP = 512
GMLP_GROUPS = 4
CHUNK = 128
D_FF = 2816
FF_CHUNK = 256
N_FF_CHUNKS = D_FF // FF_CHUNK
EPS = 1e-6
QK_SCALE = HEAD_DIM ** -0.5
KEYS_PER_QUERY_BLOCK = 128
MASKED = -1e30
LANES = 128

VMEM_LIMIT_BYTES = 60 * 1024 * 1024
ROW_TILE = 512


def _resident(shape):
    zeros = (0,) * len(shape)
    return pl.BlockSpec(shape, lambda *_: zeros, pipeline_mode=pl.Buffered(1))


def _whole(shape):
    zeros = (0,) * len(shape)
    return pl.BlockSpec(shape, lambda *_: zeros)


def _rms(x):
    return x * lax.rsqrt(jnp.mean(x * x, axis=-1, keepdims=True) + EPS)


def _mod(mod_ref, i):
    if len(mod_ref.shape) == 2:
        return mod_ref[i:i + 1, :]
    return mod_ref[i]


def _pre(x, g, mod_ref, sub):
    return (_rms(x) * g) * (1.0 + _mod(mod_ref, 3 * sub + 1)) + _mod(mod_ref, 3 * sub)


def _swiglu_chunk(c, h_ref, wu_ref, wd_ref, acc_ref):
    lo, hi = c * FF_CHUNK, (c + 1) * FF_CHUNK
    h = h_ref[...]
    a = jnp.dot(h, wu_ref[:, lo:hi], preferred_element_type=F32)
    b = jnp.dot(h, wu_ref[:, D_FF + lo:D_FF + hi], preferred_element_type=F32)
    act = ((a * jax.nn.sigmoid(a)) * b).astype(BF16)
    part = jnp.dot(act, wd_ref[lo:hi, :], preferred_element_type=F32)
    if c == 0:
        acc_ref[...] = part
    else:
        acc_ref[...] += part


def _swiglu(h_ref, wu_ref, wd_ref, acc_ref, side_tasks=()):
    dealt = 0
    for c in range(N_FF_CHUNKS):
        _swiglu_chunk(c, h_ref, wu_ref, wd_ref, acc_ref)
        while dealt * N_FF_CHUNKS < len(side_tasks) * (c + 1):
            side_tasks[dealt]()
            dealt += 1
    return acc_ref[...]


def _ada_kernel(c_ref, w_ref, b_ref, o_ref):
    c = c_ref[...]
    h = (c * jax.nn.sigmoid(c)).astype(BF16)
    o_ref[...] = jnp.dot(h, w_ref[...].astype(BF16), preferred_element_type=F32) + b_ref[...]


def _ada_table(c_all, ada_w, ada_b):
    n = c_all.shape[0]
    n_ada = ada_w.shape[1] // D_MODEL
    return pl.pallas_call(
        _ada_kernel,
        grid=(n_ada,),
        in_specs=[
            pl.BlockSpec((n, D_MODEL), lambda i: (0, 0)),
            pl.BlockSpec((D_MODEL, D_MODEL), lambda i: (0, i)),
            pl.BlockSpec((1, D_MODEL), lambda i: (0, i)),
        ],
        out_specs=pl.BlockSpec((None, n, D_MODEL), lambda i: (i, 0, 0)),
        out_shape=jax.ShapeDtypeStruct((n_ada, n, D_MODEL), F32),
        compiler_params=pltpu.CompilerParams(dimension_semantics=("arbitrary",)),
        name="k_ada",
    )(c_all, ada_w, ada_b)


def _ffn1_front(x_ref, mod_ref, ng_ref, wu_ref, wd_ref, hb_scr, acc_scr, side_tasks=()):
    x = x_ref[...]
    hb_scr[...] = _pre(x, ng_ref[0:1, :], mod_ref, 0).astype(BF16)
    f = _swiglu(hb_scr, wu_ref, wd_ref, acc_scr, side_tasks)
    x1 = x + (0.5 * _mod(mod_ref, 2)) * f
    return x1, _pre(x1, ng_ref[1:2, :], mod_ref, 1)


def _ffn_qkv_prompt_kernel(x_ref, mod_ref, ng_ref, wu_ref, wd_ref, wqkv_ref, wkvt_ref,
                           qkvt_ref, ck0_ref, ck1_ref, ck2_ref,
                           x1_ref, a0_ref, a1_ref, a2_ref, kv0_ref, kv1_ref, kv2_ref,
                           nk0_ref, nk1_ref, nk2_ref, pn0_ref, pn1_ref, pn2_ref, stats_ref,
                           hb_scr, acc_scr, h1_scr, hp_scr, *, seq_len):
    tm = x_ref.shape[0]
    i = pl.program_id(1)

    request = pl.program_id(0) * pl.num_programs(1) + i
    onehot = (lax.broadcasted_iota(jnp.int32, (1, qkvt_ref.shape[1]), 1) == request).astype(F32)
    cols = jnp.sum(qkvt_ref[0:2 * D_ATTN, :] * onehot, axis=1, keepdims=True)

    def scores_task(g, h, c_ref, n_ref, pn_ref):
        p, p_new, l, lse = _sample_scores(g, h, cols, c_ref, n_ref)
        inv_l = 1.0 / l
        pn_ref[h:h + 1, :] = p * inv_l
        row = g * HEADS_PER_GROUP + h
        stats_ref[row:row + 1, :] = jnp.broadcast_to(p_new * inv_l, (1, LANES))
        n_heads = N_GROUPS * HEADS_PER_GROUP
        stats_ref[n_heads + row:n_heads + row + 1, :] = jnp.broadcast_to(lse, (1, LANES))

    side_tasks = [functools.partial(scores_task, g, h, c_ref, n_ref, pn_ref)
                  for g, (c_ref, n_ref, pn_ref) in enumerate(((ck0_ref, nk0_ref, pn0_ref), (ck1_ref, nk1_ref, pn1_ref),
                                                              (ck2_ref, nk2_ref, pn2_ref)))
                  for h in range(HEADS_PER_GROUP)]
    x1, h1 = _ffn1_front(x_ref, mod_ref, ng_ref, wu_ref, wd_ref, hb_scr, acc_scr, side_tasks)
    x1_ref[...] = x1
    for c in range(D_MODEL // LANES):
        h1_scr[c] = h1[:, c * LANES:(c + 1) * LANES]
    hb_scr[...] = h1.astype(BF16)

    for g, a_ref in enumerate((a0_ref, a1_ref, a2_ref)):
        dil = DIL_GROUPS[g][1]
        n = tm // dil
        if dil == 1:
            hp = hb_scr[...]
        else:
            for r in range(dil):
                rows = [h1_scr[c, pl.ds(r, n, stride=dil), :] for c in range(D_MODEL // LANES)]
                hp_scr[r * n:(r + 1) * n, :] = jnp.concatenate(rows, axis=1).astype(BF16)
            hp = hp_scr[...]
        for part in range(3):
            lo = part * D_ATTN + g * GROUP_W
            res = jnp.dot(hp, wqkv_ref[:, lo:lo + GROUP_W], preferred_element_type=F32)
            if part == 0:
                res = res * QK_SCALE
            a_ref[:, :, part * GROUP_W:(part + 1) * GROUP_W] = res.astype(BF16).reshape(dil, n, GROUP_W)

    for g, kv_ref in enumerate((kv0_ref, kv1_ref, kv2_ref)):
        window = DIL_GROUPS[g][0]
        first_tile = (seq_len - window) // tm

        @pl.when(i >= first_tile)
        def _(g=g, kv_ref=kv_ref, window=window):
            for part in range(2):
                lo = part * D_ATTN + g * GROUP_W
                t = lax.dot_general(wkvt_ref[lo:lo + GROUP_W, :], hb_scr[...], (((1,), (1,)), ((), ())),
                                    preferred_element_type=F32)
                if window < tm:
                    t = t[:, tm - window:]
                kv_ref[part * HEADS_PER_GROUP:(part + 1) * HEADS_PER_GROUP] = t.reshape(
                    HEADS_PER_GROUP, HEAD_DIM, t.shape[-1])


def _ffn_qkv_prompt(x, mod, norm_g, wu, wd, wqkv, wkvt, qkvt, caches):
    bsz, seq_len, _ = x.shape
    tm = ROW_TILE
    nt = seq_len // tm
    n_req = qkvt.shape[1]
    assert bsz * nt == n_req, "one sample request rides on each row-tile grid step"
    weights = (norm_g, wu, wd, wqkv, wkvt, qkvt)
    n_heads = N_GROUPS * HEADS_PER_GROUP
    k_half = lambda c: pl.BlockSpec((None, HEADS_PER_GROUP) + c.shape[2:], lambda b, i: (b * nt + i, 0, 0, 0))
    side_in_specs = [k_half(c) for c in caches]
    side_out_specs = ([k_half(c) for c in caches]
                      + [pl.BlockSpec((None, HEADS_PER_GROUP, c.shape[-1]), lambda b, i: (b * nt + i, 0, 0))
                         for c in caches]
                      + [pl.BlockSpec((None, 2 * n_heads, LANES), lambda b, i: (b * nt + i, 0, 0))])
    side_out_shapes = ([jax.ShapeDtypeStruct(c.shape, F32) for c in caches]
                       + [jax.ShapeDtypeStruct((n_req, HEADS_PER_GROUP, c.shape[-1]), F32) for c in caches]
                       + [jax.ShapeDtypeStruct((n_req, 2 * n_heads, LANES), F32)])
    a_shapes, a_specs, kv_shapes, kv_specs = [], [], [], []
    for window, dil in DIL_GROUPS:
        a_shapes.append(jax.ShapeDtypeStruct((bsz, dil, seq_len // dil, 3 * GROUP_W), BF16))
        a_specs.append(pl.BlockSpec((None, dil, tm // dil, 3 * GROUP_W), lambda b, i: (b, 0, i, 0)))
        first_tile = (seq_len - window) // tm
        kv_shapes.append(jax.ShapeDtypeStruct((bsz, 2 * HEADS_PER_GROUP, HEAD_DIM, window), F32))
        kv_specs.append(pl.BlockSpec(
            (None, 2 * HEADS_PER_GROUP, HEAD_DIM, min(window, tm)),
            lambda b, i, first_tile=first_tile: (b, 0, 0, jnp.maximum(i - first_tile, 0))))
    row_spec = pl.BlockSpec((None, tm, D_MODEL), lambda b, i: (b, i, 0))
    return pl.pallas_call(
        functools.partial(_ffn_qkv_prompt_kernel, seq_len=seq_len),
        grid=(bsz, seq_len // tm),
        in_specs=[row_spec, pl.BlockSpec((None,) + mod.shape[1:], lambda b, i: (b, 0, 0))]
        + [_resident(w.shape) for w in weights] + side_in_specs,
        out_specs=[row_spec] + a_specs + kv_specs + side_out_specs,
        out_shape=[jax.ShapeDtypeStruct(x.shape, F32)] + a_shapes + kv_shapes + side_out_shapes,
        scratch_shapes=[pltpu.VMEM((tm, D_MODEL), BF16), pltpu.VMEM((tm, D_MODEL), F32),
                        pltpu.VMEM((D_MODEL // LANES, tm, LANES), F32), pltpu.VMEM((tm, D_MODEL), BF16)],
        compiler_params=pltpu.CompilerParams(dimension_semantics=("arbitrary", "arbitrary"),
                                             vmem_limit_bytes=VMEM_LIMIT_BYTES),
        name="k_ffn_qkv",
    )(x, mod, *weights, *caches)


def _ffn_qkv_sample_kernel(x_ref, mod_ref, ng_ref, wu_ref, wd_ref, wqkv_ref,
                           x1_ref, qkvt_ref, hb_scr, acc_scr):
    x1, h1 = _ffn1_front(x_ref, mod_ref, ng_ref, wu_ref, wd_ref, hb_scr, acc_scr)
    x1_ref[...] = x1
    qkv = jnp.dot(h1.astype(BF16), wqkv_ref[...], preferred_element_type=F32)
    col = lax.broadcasted_iota(jnp.int32, (1, 3 * D_ATTN), 1)
    qkv = qkv * jnp.where(col < D_ATTN, QK_SCALE, 1.0)
    qkvt_ref[...] = qkv.T


def _ffn_qkv_sample(x, mod, norm_g, wu, wd, wqkv):
    n = x.shape[0]
    weights = (norm_g, wu, wd, wqkv)
    return pl.pallas_call(
        _ffn_qkv_sample_kernel,
        grid=(1,),
        in_specs=[_resident(x.shape), _resident(mod.shape)] + [_resident(w.shape) for w in weights],
        out_specs=[_whole(x.shape), _whole((3 * D_ATTN, n))],
        out_shape=[jax.ShapeDtypeStruct(x.shape, F32), jax.ShapeDtypeStruct((3 * D_ATTN, n), F32)],
        scratch_shapes=[pltpu.VMEM((n, D_MODEL), BF16), pltpu.VMEM((n, D_MODEL), F32)],
        compiler_params=pltpu.CompilerParams(dimension_semantics=("arbitrary",),
                                             vmem_limit_bytes=VMEM_LIMIT_BYTES),
        name="k_ffn_qkv_sample",
    )(x, mod, *weights)


def _band_scores(q, k2, lane_head):
    qs = jnp.concatenate([jnp.where(lane_head == h, q, jnp.zeros_like(q)) for h in range(HEADS_PER_GROUP)], axis=0)
    return lax.dot_general(qs, k2, (((1,), (1,)), ((), ())), preferred_element_type=F32)


def _band_softmax_values(s, v2, first_key, band_key, lane_head, ones):
    nq = KEYS_PER_QUERY_BLOCK
    s = jnp.where(band_key >= first_key, s, MASKED)
    m = jnp.max(s, axis=1, keepdims=True)
    p = jnp.exp(s - m).astype(BF16)
    r = jnp.dot(p, v2, preferred_element_type=F32)
    l = jnp.dot(p, ones, preferred_element_type=F32)
    inv_l = 1.0 / l
    lse = m + jnp.log(l)
    o = lse_b = None
    for h in range(HEADS_PER_GROUP):
        rows = slice(h * nq, (h + 1) * nq)
        o_h = r[rows] * jnp.concatenate([inv_l[rows]] * (GROUP_W // LANES), axis=1)
        lse_h = jnp.concatenate([lse[rows]] * (GROUP_W // LANES), axis=1)
        o = o_h if h == 0 else jnp.where(lane_head == h, o_h, o)
        lse_b = lse_h if h == 0 else jnp.where(lane_head == h, lse_h, lse_b)
    return o, lse_b


def _shifted(buf, newest, new_col):
    return jnp.where(newest, new_col, pltpu.roll(buf, buf.shape[-1] - 1, axis=1))


def _sample_scores(g, h, cols, c_ref, n_ref):
    window, dil = DIL_GROUPS[g]
    rows = c_ref.shape[-1]
    col = lax.broadcasted_iota(jnp.int32, (1, rows), 1)
    dist = rows - col
    strided = (dist % dil == 0) & (dist <= window)
    base = g * GROUP_W + h * HEAD_DIM
    qc = cols[base:base + HEAD_DIM]
    kc = cols[D_ATTN + base:D_ATTN + base + HEAD_DIM]
    kt = c_ref[h]
    s = jnp.where(strided, jnp.sum(kt * qc, axis=0, keepdims=True), MASKED)
    s_new = jnp.sum(kc * qc, axis=0, keepdims=True)
    m = jnp.maximum(jnp.max(s, axis=1, keepdims=True), s_new)
    p = jnp.exp(s - m)
    p_new = jnp.exp(s_new - m)
    l = jnp.sum(p, axis=1, keepdims=True) + p_new
    n_ref[h] = _shifted(kt, col == rows - 1, kc)
    return p, p_new, l, m + jnp.log(l)


def _sample_values(g, h, v_cols, c_ref, n_ref, pn, pn_new):
    rows = c_ref.shape[-1]
    col = lax.broadcasted_iota(jnp.int32, (1, rows), 1)
    base = g * GROUP_W + h * HEAD_DIM
    vc = v_cols[base:base + HEAD_DIM]
    vt = c_ref[h]
    n_ref[h] = _shifted(vt, col == rows - 1, vc)
    return jnp.sum(vt * pn, axis=1, keepdims=True) + vc * pn_new


def _attn_kernel(qkvt_ref, a0_ref, p0_ref, a1_ref, p1_ref, a2_ref, p2_ref, c0_ref, c1_ref, c2_ref,
                 pn0_ref, pn1_ref, pn2_ref, stats_ref, nk0_ref, nk1_ref, nk2_ref,
                 ob0_ref, ob1_ref, ob2_ref, ls0_ref, ls1_ref, ls2_ref, n0_ref, n1_ref, n2_ref, attnt_ref,
                 *, blocks_per_seq):
    del nk0_ref, nk1_ref, nk2_ref
    step = pl.program_id(0)
    nq = KEYS_PER_QUERY_BLOCK

    @pl.when(step == 0)
    def _():
        attnt_ref[...] = jnp.zeros_like(attnt_ref)

    lane_head = lax.broadcasted_iota(jnp.int32, (nq, GROUP_W), 1) // HEAD_DIM
    qi = lax.broadcasted_iota(jnp.int32, (HEADS_PER_GROUP * nq, 2 * nq), 0) % nq
    kj = lax.broadcasted_iota(jnp.int32, (HEADS_PER_GROUP * nq, 2 * nq), 1)
    band_key = jnp.where((kj >= qi) & (kj <= qi + nq), kj, -1)
    ones = jnp.ones((2 * nq, LANES), BF16)

    scores = {}

    def scores_of(g, t, a_ref, p_ref, og_ref):
        before = p_ref[0] if t == 0 else a_ref[t - 1]
        own = a_ref[t]
        k2 = jnp.concatenate([before[:, GROUP_W:2 * GROUP_W], own[:, GROUP_W:2 * GROUP_W]], axis=0)
        scores[g, t] = _band_scores(own[:, 0:GROUP_W], k2, lane_head)

    def band_task(g, t, a_ref, p_ref, og_ref):
        tbb = a_ref.shape[0]
        before = p_ref[0] if t == 0 else a_ref[t - 1]
        v2 = jnp.concatenate([before[:, 2 * GROUP_W:], a_ref[t][:, 2 * GROUP_W:]], axis=0)
        first_key = jnp.where(((step * tbb + t) % blocks_per_seq[g]) == 0, nq, 0)
        o, lse_b = _band_softmax_values(scores.pop((g, t)), v2, first_key, band_key, lane_head, ones)
        ob_ref, ls_ref = og_ref
        ob_ref[t] = o.astype(BF16)
        ls_ref[t] = lse_b

    blocks = [(g, t, a_ref, p_ref, og_ref)
              for g, (a_ref, p_ref, og_ref) in enumerate(((a0_ref, p0_ref, (ob0_ref, ls0_ref)),
                                                          (a1_ref, p1_ref, (ob1_ref, ls1_ref)),
                                                          (a2_ref, p2_ref, (ob2_ref, ls2_ref))))
              for t in range(a_ref.shape[0])]
    score_ahead = 2
    for blk in blocks[:score_ahead]:
        scores_of(*blk)

    def make_band_task(i):
        def task():
            band_task(*blocks[i])
            if i + score_ahead < len(blocks):
                scores_of(*blocks[i + score_ahead])
        return task

    band_tasks = [make_band_task(i) for i in range(len(blocks))]

    n_req = qkvt_ref.shape[1]
    onehot = (lax.broadcasted_iota(jnp.int32, (1, n_req), 1) == step).astype(F32)
    v_cols = jnp.sum(qkvt_ref[2 * D_ATTN:, :] * onehot, axis=1, keepdims=True)
    n_heads = N_GROUPS * HEADS_PER_GROUP
    heads = {}

    def values_task(g, h, c_ref, n_ref, pn_ref):
        row = g * HEADS_PER_GROUP + h
        pn_new = stats_ref[row:row + 1, 0:1]
        lse = stats_ref[n_heads + row:n_heads + row + 1, 0:1]
        heads[g, h] = (_sample_values(g, h, v_cols, c_ref, n_ref, pn_ref[h:h + 1, :], pn_new), lse)

    sample_tasks = []
    for g, (c_ref, n_ref, pn_ref) in enumerate(((c0_ref, n0_ref, pn0_ref), (c1_ref, n1_ref, pn1_ref),
                                                (c2_ref, n2_ref, pn2_ref))):
        for h in range(HEADS_PER_GROUP):
            sample_tasks.append((c_ref.shape[-1], functools.partial(values_task, g, h, c_ref, n_ref, pn_ref)))

    total = sum(cost for cost, _ in sample_tasks)
    dealt = nxt = 0
    for i, task in enumerate(band_tasks):
        task()
        while nxt < len(sample_tasks) and dealt * len(band_tasks) < total * (i + 1):
            cost, sample_task = sample_tasks[nxt]
            sample_task()
            dealt += cost
            nxt += 1
    for _, sample_task in sample_tasks[nxt:]:
        sample_task()

    for h in range(HEADS_PER_GROUP):
        lses = [heads[g, h][1] for g in range(N_GROUPS)]
        m = jnp.maximum(jnp.maximum(lses[0], lses[1]), lses[2])
        es = [jnp.exp(lse - m) for lse in lses]
        inv = 1.0 / (es[0] + es[1] + es[2])
        for g in range(N_GROUPS):
            base = g * GROUP_W + h * HEAD_DIM
            attnt_ref[base:base + HEAD_DIM, :] += (heads[g, h][0] * (es[g] * inv)) * onehot


def _attention(qkvt, a_blocks, caches, half_shifted, weights, stats, blocks_per_seq):
    n_req = qkvt.shape[1]
    n_blocks = a_blocks[0].shape[0]
    assert n_blocks % n_req == 0, "query blocks are dealt evenly over the per-request grid steps"
    tbb = n_blocks // n_req
    nq = KEYS_PER_QUERY_BLOCK
    a_specs, operands = [], [qkvt]
    for a in a_blocks:
        a_specs += [pl.BlockSpec((tbb, nq, 3 * GROUP_W), lambda r: (r, 0, 0)),
                    pl.BlockSpec((1, nq, 3 * GROUP_W), lambda r: (jnp.maximum(r * tbb - 1, 0), 0, 0))]
        operands += [a, a]
    v_specs = [pl.BlockSpec((None, HEADS_PER_GROUP) + c.shape[2:], lambda r: (r, 1, 0, 0)) for c in caches]
    w_specs = [pl.BlockSpec((None,) + w.shape[1:], lambda r: (r, 0, 0)) for w in weights]
    stats_spec = pl.BlockSpec((None,) + stats.shape[1:], lambda r: (r, 0, 0))
    og_spec = pl.BlockSpec((tbb, nq, GROUP_W), lambda r: (r, 0, 0))
    operands += [*caches, *weights, stats, *half_shifted]
    first_aliased = len(operands) - len(half_shifted)
    return pl.pallas_call(
        functools.partial(_attn_kernel, blocks_per_seq=blocks_per_seq),
        grid=(n_req,),
        in_specs=[_resident(qkvt.shape)] + a_specs + v_specs + w_specs + [stats_spec]
        + [pl.BlockSpec(memory_space=pl.ANY)] * len(half_shifted),
        out_specs=[og_spec] * (2 * N_GROUPS) + v_specs + [pl.BlockSpec((D_ATTN, n_req), lambda r: (0, 0))],
        out_shape=[jax.ShapeDtypeStruct((n_blocks, nq, GROUP_W), BF16)] * N_GROUPS
        + [jax.ShapeDtypeStruct((n_blocks, nq, GROUP_W), F32)] * N_GROUPS
        + [jax.ShapeDtypeStruct(c.shape, F32) for c in caches]
        + [jax.ShapeDtypeStruct((D_ATTN, n_req), F32)],
        input_output_aliases={first_aliased + g: 2 * N_GROUPS + g for g in range(len(half_shifted))},
        compiler_params=pltpu.CompilerParams(dimension_semantics=("arbitrary",),
                                             vmem_limit_bytes=VMEM_LIMIT_BYTES),
        name="k_attn",
    )(*operands)


def _merge_groups(o_and_lse):
    m = o_and_lse[0][1]
    for _, lse in o_and_lse[1:]:
        m = jnp.maximum(m, lse)
    es = [jnp.exp(lse - m) for _, lse in o_and_lse]
    inv = 1.0 / (es[0] + es[1] + es[2])
    return jnp.concatenate([(o * (e * inv)).astype(BF16) for (o, _), e in zip(o_and_lse, es)], axis=1)


def _mix_tail(x1, gg, u, mixed, attn, mod_ref, ng_ref, wba_ref, wbb_ref, wo_ref,
              wu_ref, wd_ref, fg_ref, hb_scr, acc_scr):
    gm = (u * mixed).astype(BF16)
    merged = (jax.nn.sigmoid(gg[:, :D_MODEL]) * jnp.dot(attn, wba_ref[...], preferred_element_type=F32)
              + jax.nn.sigmoid(gg[:, D_MODEL:]) * jnp.dot(gm, wbb_ref[...], preferred_element_type=F32))
    y = jnp.dot(merged.astype(BF16), wo_ref[...], preferred_element_type=F32)
    x2 = x1 + _mod(mod_ref, 5) * y
    hb_scr[...] = _pre(x2, ng_ref[2:3, :], mod_ref, 2).astype(BF16)
    f = _swiglu(hb_scr, wu_ref, wd_ref, acc_scr)
    x3 = x2 + (0.5 * _mod(mod_ref, 8)) * f
    return _rms(x3) * fg_ref[...]


def _layer_norm_rows(v, g_ref, b_ref):
    mu = jnp.mean(v, axis=-1, keepdims=True)
    var = jnp.mean(jnp.square(v - mu), axis=-1, keepdims=True)
    return ((v - mu) * lax.rsqrt(var + EPS)) * g_ref[...] + b_ref[...]


def _mix_ffn_prompt_kernel(x1_ref, ob0_ref, ob1_ref, ob2_ref, ls0_ref, ls1_ref, ls2_ref, mod_ref, ng_ref, wr_ref,
                           lng_ref, lnb_ref,
                           ws_ref, bst_ref, wba_ref, wbb_ref, wo_ref, wu_ref, wd_ref, fg_ref,
                           y_ref, vrows_ref, hb_scr, acc_scr, un_scr, mixed_scr):
    tm = x1_ref.shape[0]
    x1 = x1_ref[...]
    h1 = _pre(x1, ng_ref[1:2, :], mod_ref, 1).astype(BF16)
    uv = jnp.dot(h1, wr_ref[:, :2 * D_GMLP], preferred_element_type=F32)
    gg = jnp.dot(h1, wr_ref[:, 2 * D_GMLP:], preferred_element_type=F32)
    u = uv[:, :D_GMLP]
    vn = _layer_norm_rows(uv[:, D_GMLP:], lng_ref, lnb_ref)

    vrows_ref[...] = vn[tm - CHUNK:, :]

    vnb = vn.astype(BF16)
    tri = (lax.broadcasted_iota(jnp.int32, (CHUNK, CHUNK), 0)
           >= lax.broadcasted_iota(jnp.int32, (CHUNK, CHUNK), 1))
    gw = D_GMLP // GMLP_GROUPS
    for g in range(GMLP_GROUPS):
        ws = jnp.where(tri, ws_ref[g], 0.0).astype(BF16)
        bias = bst_ref[:, g:g + 1]
        for c in range(tm // CHUNK):
            mixed_scr[c * CHUNK:(c + 1) * CHUNK, g * gw:(g + 1) * gw] = (
                jnp.dot(ws, vnb[c * CHUNK:(c + 1) * CHUNK, g * gw:(g + 1) * gw],
                        preferred_element_type=F32) + bias)

    half = GROUP_W // LANES
    for g, (ob_ref, lse_ref) in enumerate(((ob0_ref, ls0_ref), (ob1_ref, ls1_ref), (ob2_ref, ls2_ref))):
        dil = DIL_GROUPS[g][1]
        n = tm // dil
        for r in range(dil):
            for c in range(half):
                lanes = slice(c * LANES, (c + 1) * LANES)
                un_scr[g, c, pl.ds(r, n, stride=dil), :] = ob_ref[r, :, lanes].astype(F32)
                un_scr[g, half + c, pl.ds(r, n, stride=dil), :] = lse_ref[r, :, lanes]
    o_and_lse = []
    for g in range(N_GROUPS):
        cols = [un_scr[g, c] for c in range(2 * half)]
        o_and_lse.append((jnp.concatenate(cols[:half], axis=1), jnp.concatenate(cols[half:], axis=1)))
    attn = _merge_groups(o_and_lse)

    y_ref[...] = _mix_tail(x1, gg, u, mixed_scr[...], attn, mod_ref, ng_ref, wba_ref, wbb_ref, wo_ref,
                           wu_ref, wd_ref, fg_ref, hb_scr, acc_scr)


def _mix_ffn_prompt(x1, obs, lses, mod, norm_g, wr, lng, lnb, ws, bst, wba, wbb, wo, wu, wd, fg):
    bsz, seq_len, _ = x1.shape
    tm = ROW_TILE
    weights = (norm_g, wr, lng, lnb, ws, bst, wba, wbb, wo, wu, wd, fg)
    row_spec = pl.BlockSpec((None, tm, D_MODEL), lambda b, i: (b, i, 0))
    og_specs = [pl.BlockSpec((None, dil, tm // dil, GROUP_W), lambda b, i: (b, 0, i, 0))
                for _, dil in DIL_GROUPS] * 2
    return pl.pallas_call(
        _mix_ffn_prompt_kernel,
        grid=(bsz, seq_len // tm),
        in_specs=[row_spec] + og_specs + [pl.BlockSpec((None,) + mod.shape[1:], lambda b, i: (b, 0, 0))]
        + [_resident(w.shape) for w in weights],
        out_specs=[row_spec, pl.BlockSpec((None, CHUNK, D_GMLP), lambda b, i: (b, 0, 0))],
        out_shape=[jax.ShapeDtypeStruct(x1.shape, F32), jax.ShapeDtypeStruct((bsz, CHUNK, D_GMLP), F32)],
        scratch_shapes=[pltpu.VMEM((tm, D_MODEL), BF16), pltpu.VMEM((tm, D_MODEL), F32),
                        pltpu.VMEM((N_GROUPS, 2 * GROUP_W // LANES, tm, LANES), F32),
                        pltpu.VMEM((tm, D_GMLP), F32)],
        compiler_params=pltpu.CompilerParams(dimension_semantics=("arbitrary", "arbitrary"),
                                             vmem_limit_bytes=VMEM_LIMIT_BYTES),
        name="k_mix_ffn",
    )(x1, *obs, *lses, mod, *weights)


def _mix_ffn_sample_kernel(x1_ref, attnt_ref, mod_ref, ng_ref, wr_ref, lng_ref, lnb_ref,
                           ws0_ref, bs0_ref, wba_ref, wbb_ref, wo_ref, wu_ref, wd_ref, fg_ref,
                           y_ref, vrows_ref, hb_scr, acc_scr):
    x1 = x1_ref[...]
    h1 = _pre(x1, ng_ref[1:2, :], mod_ref, 1).astype(BF16)
    uv = jnp.dot(h1, wr_ref[:, :2 * D_GMLP], preferred_element_type=F32)
    gg = jnp.dot(h1, wr_ref[:, 2 * D_GMLP:], preferred_element_type=F32)
    u = uv[:, :D_GMLP]
    vn = _layer_norm_rows(uv[:, D_GMLP:], lng_ref, lnb_ref)
    vrows_ref[...] = vn
    mixed = ws0_ref[...] * vn + bs0_ref[...]
    attn = attnt_ref[...].T.astype(BF16)
    y_ref[...] = _mix_tail(x1, gg, u, mixed, attn, mod_ref, ng_ref, wba_ref, wbb_ref, wo_ref,
                           wu_ref, wd_ref, fg_ref, hb_scr, acc_scr)


def _mix_ffn_sample(x1, attnt, mod, norm_g, wr, lng, lnb, ws0, bs0, wba, wbb, wo, wu, wd, fg):
    n = x1.shape[0]
    operands = (x1, attnt, mod, norm_g, wr, lng, lnb, ws0, bs0, wba, wbb, wo, wu, wd, fg)
    return pl.pallas_call(
        _mix_ffn_sample_kernel,
        grid=(1,),
        in_specs=[_resident(w.shape) for w in operands],
        out_specs=[_whole(x1.shape), _whole((n, D_GMLP))],
        out_shape=[jax.ShapeDtypeStruct(x1.shape, F32), jax.ShapeDtypeStruct((n, D_GMLP), F32)],
        scratch_shapes=[pltpu.VMEM((n, D_MODEL), BF16), pltpu.VMEM((n, D_MODEL), F32)],
        compiler_params=pltpu.CompilerParams(dimension_semantics=("arbitrary",),
                                             vmem_limit_bytes=VMEM_LIMIT_BYTES),
        name="k_mix_ffn_sample",
    )(*operands)


def _to_buffer_layout(cache):
    n, rows = cache.shape[0], cache.shape[1]
    return cache.transpose(0, 2, 3, 4, 1).reshape(n, 2 * HEADS_PER_GROUP, HEAD_DIM, rows)


def _from_buffer_layout(buf):
    n, rows = buf.shape[0], buf.shape[-1]
    return buf.reshape(n, 2, HEADS_PER_GROUP, HEAD_DIM, rows).transpose(0, 4, 1, 2, 3)[None]


def kernel(x_prompt, x_sample, c_prompt, c_sample, cache_kv_g0, cache_kv_g1, cache_kv_g2, ada_w, ada_b, norm_g,
           ffn1_up, ffn1_down, w_in, w_branch_a, w_branch_b, w_out, v_ln_g, v_ln_b, w_spatial, b_spatial,
           ffn2_up, ffn2_down, final_g):
    assert ada_w.shape[0] == 1, "one layer"
    bsz, seq_len, _ = x_prompt.shape
    n_req = x_sample.shape[0]

    wu1, wd1 = ffn1_up[0].astype(BF16), ffn1_down[0].astype(BF16)
    wu2, wd2 = ffn2_up[0].astype(BF16), ffn2_down[0].astype(BF16)
    wqkv = w_in[0][:, :3 * D_ATTN].astype(BF16)
    wr = w_in[0][:, 3 * D_ATTN:].astype(BF16)
    wkvt = wqkv[:, D_ATTN:].T
    wba, wbb, wo = w_branch_a[0].astype(BF16), w_branch_b[0].astype(BF16), w_out[0].astype(BF16)
    ng = norm_g[0]
    lng, lnb = v_ln_g, v_ln_b
    ws = w_spatial[0]
    bst = b_spatial[0].T
    gw = D_GMLP // GMLP_GROUPS
    ws0 = jnp.repeat(ws[:, 0, 0], gw)[None, :]
    bs0 = jnp.repeat(b_spatial[0][:, 0], gw)[None, :]
    fg = final_g[None, :]

    mods = _ada_table(jnp.concatenate([c_prompt, c_sample], axis=0), ada_w[0], ada_b)
    mod_p = mods[:, :bsz].transpose(1, 0, 2)
    mod_s = mods[:, bsz:]

    x1s, qkvt = _ffn_qkv_sample(x_sample.reshape(n_req, D_MODEL), mod_s, ng, wu1, wd1, wqkv)
    caches = [_to_buffer_layout(c[0]) for c in (cache_kv_g0, cache_kv_g1, cache_kv_g2)]
    (x1, a0, a1, a2, kv0, kv1, kv2, nk0, nk1, nk2, pn0, pn1, pn2, stats) = _ffn_qkv_prompt(
        x_prompt, mod_p, ng, wu1, wd1, wqkv, wkvt, qkvt, caches)

    nq = KEYS_PER_QUERY_BLOCK
    a_blocks = [a.reshape(-1, nq, 3 * GROUP_W) for a in (a0, a1, a2)]
    blocks_per_seq = tuple(seq_len // dil // nq for _, dil in DIL_GROUPS)
    ob0, ob1, ob2, ls0, ls1, ls2, s0, s1, s2, attnt = _attention(
        qkvt, a_blocks, caches, (nk0, nk1, nk2), (pn0, pn1, pn2), stats, blocks_per_seq)
    per_residue = lambda og, dil: og.reshape(bsz, dil, seq_len // dil, GROUP_W)
    obs = [per_residue(og, dil) for og, (_, dil) in zip((ob0, ob1, ob2), DIL_GROUPS)]
    lses = [per_residue(og, dil) for og, (_, dil) in zip((ls0, ls1, ls2), DIL_GROUPS)]

    y_prompt, vrows_p = _mix_ffn_prompt(x1, obs, lses, mod_p, ng, wr, lng, lnb, ws, bst, wba, wbb, wo,
                                        wu2, wd2, fg)
    y_sample, vrows_s = _mix_ffn_sample(x1s, attnt, mod_s, ng, wr, lng, lnb, ws0, bs0, wba, wbb, wo,
                                        wu2, wd2, fg)

    return (y_prompt, y_sample.reshape(n_req, 1, D_MODEL),
            _from_buffer_layout(kv0), _from_buffer_layout(kv1), _from_buffer_layout(kv2), vrows_p[None],
            _from_buffer_layout(s0), _from_buffer_layout(s1), _from_buffer_layout(s2),
            vrows_s.reshape(1, n_req, 1, D_GMLP))
```

```python
import functools

import jax
import jax.numpy as jnp
from jax import lax
from jax.experimental import pallas as pl
from jax.experimental.pallas import tpu as pltpu

F32 = jnp.float32
BF16 = jnp.bfloat16

D_MODEL = 1024
HEAD_DIM = 64
HEADS_PER_GROUP = 4
HEADS_PER_PRODUCT = 2
GROUP_W = HEADS_PER_GROUP * HEAD_DIM
DIL_GROUPS = ((128, 1), (512, 4), (2048, 16))
N_GROUPS = len(DIL_GROUPS)
D_ATTN = N_GROUPS * GROUP_W
D_GMLP = 512
GMLP_GROUPS = 4
CHUNK = 128
D_FF = 2816
FF_CHUNK = 256
N_FF_CHUNKS = D_FF // FF_CHUNK
EPS = 1e-6
QK_SCALE = HEAD_DIM ** -0.5
KEYS_PER_QUERY_BLOCK = 128
MASKED = -1e30
LANES = 128

VMEM_LIMIT_BYTES = 60 * 1024 * 1024
ROW_TILE = 512


def _resident(shape):
    zeros = (0,) * len(shape)
    return pl.BlockSpec(shape, lambda *_: zeros, pipeline_mode=pl.Buffered(1))


def _whole(shape):
    zeros = (0,) * len(shape)
    return pl.BlockSpec(shape, lambda *_: zeros)


def _rms(x):
    return x * lax.rsqrt(jnp.mean(x * x, axis=-1, keepdims=True) + EPS)


def _mod(mod_ref, i):
    if len(mod_ref.shape) == 2:
        return mod_ref[i:i + 1, :]
    return mod_ref[i]


def _pre(x, g, mod_ref, sub):
    return (_rms(x) * g) * (1.0 + _mod(mod_ref, 3 * sub + 1)) + _mod(mod_ref, 3 * sub)


def _swiglu_chunk(c, h_ref, wu_ref, wd_ref, acc_ref):
    lo, hi = c * FF_CHUNK, (c + 1) * FF_CHUNK
    h = h_ref[...]
    a = jnp.dot(h, wu_ref[:, lo:hi], preferred_element_type=F32)
    b = jnp.dot(h, wu_ref[:, D_FF + lo:D_FF + hi], preferred_element_type=F32)
    act = ((a * jax.nn.sigmoid(a)) * b).astype(BF16)
    part = jnp.dot(act, wd_ref[lo:hi, :], preferred_element_type=F32)
    if c == 0:
        acc_ref[...] = part
    else:
        acc_ref[...] += part


def _swiglu(h_ref, wu_ref, wd_ref, acc_ref, side_tasks=()):
    dealt = 0
    for c in range(N_FF_CHUNKS):
        _swiglu_chunk(c, h_ref, wu_ref, wd_ref, acc_ref)
        while dealt * N_FF_CHUNKS < len(side_tasks) * (c + 1):
            side_tasks[dealt]()
            dealt += 1
    return acc_ref[...]


def _ada_kernel(c_ref, w_ref, b_ref, o_ref):
    c = c_ref[...]
    h = (c * jax.nn.sigmoid(c)).astype(BF16)
    o_ref[...] = jnp.dot(h, w_ref[...].astype(BF16), preferred_element_type=F32) + b_ref[...]


def _ada_table(c_all, ada_w, ada_b):
    n = c_all.shape[0]
    n_ada = ada_w.shape[1] // D_MODEL
    return pl.pallas_call(
        _ada_kernel,
        grid=(n_ada,),
        in_specs=[
            pl.BlockSpec((n, D_MODEL), lambda i: (0, 0)),
            pl.BlockSpec((D_MODEL, D_MODEL), lambda i: (0, i)),
            pl.BlockSpec((1, D_MODEL), lambda i: (0, i)),
        ],
        out_specs=pl.BlockSpec((None, n, D_MODEL), lambda i: (i, 0, 0)),
        out_shape=jax.ShapeDtypeStruct((n_ada, n, D_MODEL), F32),
        compiler_params=pltpu.CompilerParams(dimension_semantics=("arbitrary",)),
        name="k_ada",
    )(c_all, ada_w, ada_b)


def _ffn1_front(x_ref, mod_ref, ng_ref, wu_ref, wd_ref, hb_scr, acc_scr, side_tasks=()):
    x = x_ref[...]
    hb_scr[...] = _pre(x, ng_ref[0:1, :], mod_ref, 0).astype(BF16)
    f = _swiglu(hb_scr, wu_ref, wd_ref, acc_scr, side_tasks)
    x1 = x + (0.5 * _mod(mod_ref, 2)) * f
    return x1, _pre(x1, ng_ref[1:2, :], mod_ref, 1)


def _ffn_qkv_prompt_kernel(x_ref, mod_ref, ng_ref, wu_ref, wd_ref, wqkv_ref, wkvt_ref,
                           qkvt_ref, ck0_ref, ck1_ref, ck2_ref,
                           x1_ref, a0_ref, a1_ref, a2_ref, kv0_ref, kv1_ref, kv2_ref,
                           nk0_ref, nk1_ref, nk2_ref, pn0_ref, pn1_ref, pn2_ref, stats_ref,
                           hb_scr, acc_scr, h1_scr, hp_scr, *, seq_len):
    tm = x_ref.shape[0]
    i = pl.program_id(1)

    request = pl.program_id(0) * pl.num_programs(1) + i
    onehot = (lax.broadcasted_iota(jnp.int32, (1, qkvt_ref.shape[1]), 1) == request).astype(F32)
    cols = jnp.sum(qkvt_ref[0:2 * D_ATTN, :] * onehot, axis=1, keepdims=True)

    def scores_task(g, h, c_ref, n_ref, pn_ref):
        p, p_new, l, lse = _sample_scores(g, h, cols, c_ref, n_ref)
        inv_l = 1.0 / l
        pn_ref[h:h + 1, :] = p * inv_l
        row = g * HEADS_PER_GROUP + h
        stats_ref[row:row + 1, :] = jnp.broadcast_to(p_new * inv_l, (1, LANES))
        n_heads = N_GROUPS * HEADS_PER_GROUP
        stats_ref[n_heads + row:n_heads + row + 1, :] = jnp.broadcast_to(lse, (1, LANES))

    side_tasks = [functools.partial(scores_task, g, h, c_ref, n_ref, pn_ref)
                  for g, (c_ref, n_ref, pn_ref) in enumerate(((ck0_ref, nk0_ref, pn0_ref), (ck1_ref, nk1_ref, pn1_ref),
                                                              (ck2_ref, nk2_ref, pn2_ref)))
                  for h in range(HEADS_PER_GROUP)]
    x1, h1 = _ffn1_front(x_ref, mod_ref, ng_ref, wu_ref, wd_ref, hb_scr, acc_scr, side_tasks)
    x1_ref[...] = x1
    for c in range(D_MODEL // LANES):
        h1_scr[c] = h1[:, c * LANES:(c + 1) * LANES]
    hb_scr[...] = h1.astype(BF16)

    for g, a_ref in enumerate((a0_ref, a1_ref, a2_ref)):
        dil = DIL_GROUPS[g][1]
        n = tm // dil
        if dil == 1:
            hp = hb_scr[...]
        else:
            for r in range(dil):
                rows = [h1_scr[c, pl.ds(r, n, stride=dil), :] for c in range(D_MODEL // LANES)]
                hp_scr[r * n:(r + 1) * n, :] = jnp.concatenate(rows, axis=1).astype(BF16)
            hp = hp_scr[...]
        for part in range(3):
            lo = part * D_ATTN + g * GROUP_W
            res = jnp.dot(hp, wqkv_ref[:, lo:lo + GROUP_W], preferred_element_type=F32)
            if part == 0:
                res = res * QK_SCALE
            a_ref[:, :, part * GROUP_W:(part + 1) * GROUP_W] = res.astype(BF16).reshape(dil, n, GROUP_W)

    for g, kv_ref in enumerate((kv0_ref, kv1_ref, kv2_ref)):
        window = DIL_GROUPS[g][0]
        first_tile = (seq_len - window) // tm

        @pl.when(i >= first_tile)
        def _(g=g, kv_ref=kv_ref, window=window):
            for part in range(2):
                lo = part * D_ATTN + g * GROUP_W
                t = lax.dot_general(wkvt_ref[lo:lo + GROUP_W, :], hb_scr[...], (((1,), (1,)), ((), ())),
                                    preferred_element_type=F32)
                if window < tm:
                    t = t[:, tm - window:]
                kv_ref[part * HEADS_PER_GROUP:(part + 1) * HEADS_PER_GROUP] = t.reshape(
                    HEADS_PER_GROUP, HEAD_DIM, t.shape[-1])


def _ffn_qkv_prompt(x, mod, norm_g, wu, wd, wqkv, wkvt, qkvt, caches):
    bsz, seq_len, _ = x.shape
    tm = ROW_TILE
    nt = seq_len // tm
    n_req = qkvt.shape[1]
    assert bsz * nt == n_req, "one sample request rides on each row-tile grid step"
    weights = (norm_g, wu, wd, wqkv, wkvt, qkvt)
    n_heads = N_GROUPS * HEADS_PER_GROUP
    k_half = lambda c: pl.BlockSpec((None, HEADS_PER_GROUP) + c.shape[2:], lambda b, i: (b * nt + i, 0, 0, 0))
    side_in_specs = [k_half(c) for c in caches]
    side_out_specs = ([k_half(c) for c in caches]
                      + [pl.BlockSpec((None, HEADS_PER_GROUP, c.shape[-1]), lambda b, i: (b * nt + i, 0, 0))
                         for c in caches]
                      + [pl.BlockSpec((None, 2 * n_heads, LANES), lambda b, i: (b * nt + i, 0, 0))])
    side_out_shapes = ([jax.ShapeDtypeStruct(c.shape, F32) for c in caches]
                       + [jax.ShapeDtypeStruct((n_req, HEADS_PER_GROUP, c.shape[-1]), F32) for c in caches]
                       + [jax.ShapeDtypeStruct((n_req, 2 * n_heads, LANES), F32)])
    a_shapes, a_specs, kv_shapes, kv_specs = [], [], [], []
    for window, dil in DIL_GROUPS:
        a_shapes.append(jax.ShapeDtypeStruct((bsz, dil, seq_len // dil, 3 * GROUP_W), BF16))
        a_specs.append(pl.BlockSpec((None, dil, tm // dil, 3 * GROUP_W), lambda b, i: (b, 0, i, 0)))
        first_tile = (seq_len - window) // tm
        kv_shapes.append(jax.ShapeDtypeStruct((bsz, 2 * HEADS_PER_GROUP, HEAD_DIM, window), F32))
        kv_specs.append(pl.BlockSpec(
            (None, 2 * HEADS_PER_GROUP, HEAD_DIM, min(window, tm)),
            lambda b, i, first_tile=first_tile: (b, 0, 0, jnp.maximum(i - first_tile, 0))))
    row_spec = pl.BlockSpec((None, tm, D_MODEL), lambda b, i: (b, i, 0))
    return pl.pallas_call(
        functools.partial(_ffn_qkv_prompt_kernel, seq_len=seq_len),
        grid=(bsz, seq_len // tm),
        in_specs=[row_spec, pl.BlockSpec((None,) + mod.shape[1:], lambda b, i: (b, 0, 0))]
        + [_resident(w.shape) for w in weights] + side_in_specs,
        out_specs=[row_spec] + a_specs + kv_specs + side_out_specs,
        out_shape=[jax.ShapeDtypeStruct(x.shape, F32)] + a_shapes + kv_shapes + side_out_shapes,
        scratch_shapes=[pltpu.VMEM((tm, D_MODEL), BF16), pltpu.VMEM((tm, D_MODEL), F32),
                        pltpu.VMEM((D_MODEL // LANES, tm, LANES), F32), pltpu.VMEM((tm, D_MODEL), BF16)],
        compiler_params=pltpu.CompilerParams(dimension_semantics=("arbitrary", "arbitrary"),
                                             vmem_limit_bytes=VMEM_LIMIT_BYTES),
        name="k_ffn_qkv",
    )(x, mod, *weights, *caches)


def _ffn_qkv_sample_kernel(x_ref, mod_ref, ng_ref, wu_ref, wd_ref, wqkv_ref,
                           x1_ref, qkvt_ref, hb_scr, acc_scr):
    x1, h1 = _ffn1_front(x_ref, mod_ref, ng_ref, wu_ref, wd_ref, hb_scr, acc_scr)
    x1_ref[...] = x1
    qkv = jnp.dot(h1.astype(BF16), wqkv_ref[...], preferred_element_type=F32)
    col = lax.broadcasted_iota(jnp.int32, (1, 3 * D_ATTN), 1)
    qkv = qkv * jnp.where(col < D_ATTN, QK_SCALE, 1.0)
    qkvt_ref[...] = qkv.T


def _ffn_qkv_sample(x, mod, norm_g, wu, wd, wqkv):
    n = x.shape[0]
    weights = (norm_g, wu, wd, wqkv)
    return pl.pallas_call(
        _ffn_qkv_sample_kernel,
        grid=(1,),
        in_specs=[_resident(x.shape), _resident(mod.shape)] + [_resident(w.shape) for w in weights],
        out_specs=[_whole(x.shape), _whole((3 * D_ATTN, n))],
        out_shape=[jax.ShapeDtypeStruct(x.shape, F32), jax.ShapeDtypeStruct((3 * D_ATTN, n), F32)],
        scratch_shapes=[pltpu.VMEM((n, D_MODEL), BF16), pltpu.VMEM((n, D_MODEL), F32)],
        compiler_params=pltpu.CompilerParams(dimension_semantics=("arbitrary",),
                                             vmem_limit_bytes=VMEM_LIMIT_BYTES),
        name="k_ffn_qkv_sample",
    )(x, mod, *weights)


def _band_scores(q, k2, lane_head):
    pairs = []
    for first in range(0, HEADS_PER_GROUP, HEADS_PER_PRODUCT):
        qs = jnp.concatenate([jnp.where(lane_head == h, q, jnp.zeros_like(q))
                              for h in range(first, first + HEADS_PER_PRODUCT)], axis=0)
        pairs.append(lax.dot_general(qs, k2, (((1,), (1,)), ((), ())), preferred_element_type=F32))
    return pairs


def _band_softmax_values(score_pairs, v2, first_key, band_key, lane_head, ones):
    nq = KEYS_PER_QUERY_BLOCK
    o = lse_b = None
    for pair, s in enumerate(score_pairs):
        s = jnp.where(band_key >= first_key, s, MASKED)
        m = jnp.max(s, axis=1, keepdims=True)
        p = jnp.exp(s - m).astype(BF16)
        r = jnp.dot(p, v2, preferred_element_type=F32)
        l = jnp.dot(p, ones, preferred_element_type=F32)
        inv_l = 1.0 / l
        lse = m + jnp.log(l)
        for j in range(HEADS_PER_PRODUCT):
            h = pair * HEADS_PER_PRODUCT + j
            rows = slice(j * nq, (j + 1) * nq)
            o_h = r[rows] * jnp.concatenate([inv_l[rows]] * (GROUP_W // LANES), axis=1)
            lse_h = jnp.concatenate([lse[rows]] * (GROUP_W // LANES), axis=1)
            o = o_h if h == 0 else jnp.where(lane_head == h, o_h, o)
            lse_b = lse_h if h == 0 else jnp.where(lane_head == h, lse_h, lse_b)
    return o, lse_b


def _shifted(buf, newest, new_col):
    return jnp.where(newest, new_col, pltpu.roll(buf, buf.shape[-1] - 1, axis=1))


def _sample_scores(g, h, cols, c_ref, n_ref):
    window, dil = DIL_GROUPS[g]
    rows = c_ref.shape[-1]
    col = lax.broadcasted_iota(jnp.int32, (1, rows), 1)
    dist = rows - col
    strided = (dist % dil == 0) & (dist <= window)
    base = g * GROUP_W + h * HEAD_DIM
    qc = cols[base:base + HEAD_DIM]
    kc = cols[D_ATTN + base:D_ATTN + base + HEAD_DIM]
    kt = c_ref[h]
    s = jnp.where(strided, jnp.sum(kt * qc, axis=0, keepdims=True), MASKED)
    s_new = jnp.sum(kc * qc, axis=0, keepdims=True)
    m = jnp.maximum(jnp.max(s, axis=1, keepdims=True), s_new)
    p = jnp.exp(s - m)
    p_new = jnp.exp(s_new - m)
    l = jnp.sum(p, axis=1, keepdims=True) + p_new
    n_ref[h] = _shifted(kt, col == rows - 1, kc)
    return p, p_new, l, m + jnp.log(l)


def _sample_values(g, h, v_cols, c_ref, n_ref, pn, pn_new):
    rows = c_ref.shape[-1]
    col = lax.broadcasted_iota(jnp.int32, (1, rows), 1)
    base = g * GROUP_W + h * HEAD_DIM
    vc = v_cols[base:base + HEAD_DIM]
    vt = c_ref[h]
    n_ref[h] = _shifted(vt, col == rows - 1, vc)
    return jnp.sum(vt * pn, axis=1, keepdims=True) + vc * pn_new


def _attn_kernel(qkvt_ref, a0_ref, p0_ref, a1_ref, p1_ref, a2_ref, p2_ref, c0_ref, c1_ref, c2_ref,
                 pn0_ref, pn1_ref, pn2_ref, stats_ref, nk0_ref, nk1_ref, nk2_ref,
                 ob0_ref, ob1_ref, ob2_ref, ls0_ref, ls1_ref, ls2_ref, n0_ref, n1_ref, n2_ref, attnt_ref,
                 *, blocks_per_seq):
    del nk0_ref, nk1_ref, nk2_ref
    step = pl.program_id(0)
    nq = KEYS_PER_QUERY_BLOCK

    @pl.when(step == 0)
    def _():
        attnt_ref[...] = jnp.zeros_like(attnt_ref)

    lane_head = lax.broadcasted_iota(jnp.int32, (nq, GROUP_W), 1) // HEAD_DIM
    qi = lax.broadcasted_iota(jnp.int32, (HEADS_PER_PRODUCT * nq, 2 * nq), 0) % nq
    kj = lax.broadcasted_iota(jnp.int32, (HEADS_PER_PRODUCT * nq, 2 * nq), 1)
    band_key = jnp.where((kj >= qi) & (kj <= qi + nq), kj, -1)
    ones = jnp.ones((2 * nq, LANES), BF16)

    scores = {}

    def scores_of(g, t, a_ref, p_ref, og_ref):
        before = p_ref[0] if t == 0 else a_ref[t - 1]
        own = a_ref[t]
        k2 = jnp.concatenate([before[:, GROUP_W:2 * GROUP_W], own[:, GROUP_W:2 * GROUP_W]], axis=0)
        scores[g, t] = _band_scores(own[:, 0:GROUP_W], k2, lane_head)

    def band_task(g, t, a_ref, p_ref, og_ref):
        tbb = a_ref.shape[0]
        before = p_ref[0] if t == 0 else a_ref[t - 1]
        v2 = jnp.concatenate([before[:, 2 * GROUP_W:], a_ref[t][:, 2 * GROUP_W:]], axis=0)
        first_key = jnp.where(((step * tbb + t) % blocks_per_seq[g]) == 0, nq, 0)
        o, lse_b = _band_softmax_values(scores.pop((g, t)), v2, first_key, band_key, lane_head, ones)
        ob_ref, ls_ref = og_ref
        ob_ref[t] = o.astype(BF16)
        ls_ref[t] = lse_b

    blocks = [(g, t, a_ref, p_ref, og_ref)
              for g, (a_ref, p_ref, og_ref) in enumerate(((a0_ref, p0_ref, (ob0_ref, ls0_ref)),
                                                          (a1_ref, p1_ref, (ob1_ref, ls1_ref)),
                                                          (a2_ref, p2_ref, (ob2_ref, ls2_ref))))
              for t in range(a_ref.shape[0])]
    score_ahead = 2
    for blk in blocks[:score_ahead]:
        scores_of(*blk)

    def make_band_task(i):
        def task():
            band_task(*blocks[i])
            if i + score_ahead < len(blocks):
                scores_of(*blocks[i + score_ahead])
        return task

    band_tasks = [make_band_task(i) for i in range(len(blocks))]

    n_req = qkvt_ref.shape[1]
    onehot = (lax.broadcasted_iota(jnp.int32, (1, n_req), 1) == step).astype(F32)
    v_cols = jnp.sum(qkvt_ref[2 * D_ATTN:, :] * onehot, axis=1, keepdims=True)
    n_heads = N_GROUPS * HEADS_PER_GROUP
    heads = {}

    def values_task(g, h, c_ref, n_ref, pn_ref):
        row = g * HEADS_PER_GROUP + h
        pn_new = stats_ref[row:row + 1, 0:1]
        lse = stats_ref[n_heads + row:n_heads + row + 1, 0:1]
        heads[g, h] = (_sample_values(g, h, v_cols, c_ref, n_ref, pn_ref[h:h + 1, :], pn_new), lse)

    sample_tasks = []
    for g, (c_ref, n_ref, pn_ref) in enumerate(((c0_ref, n0_ref, pn0_ref), (c1_ref, n1_ref, pn1_ref),
                                                (c2_ref, n2_ref, pn2_ref))):
        for h in range(HEADS_PER_GROUP):
            sample_tasks.append((c_ref.shape[-1], functools.partial(values_task, g, h, c_ref, n_ref, pn_ref)))

    total = sum(cost for cost, _ in sample_tasks)
    dealt = nxt = 0
    for i, task in enumerate(band_tasks):
        task()
        while nxt < len(sample_tasks) and dealt * len(band_tasks) < total * (i + 1):
            cost, sample_task = sample_tasks[nxt]
            sample_task()
            dealt += cost
            nxt += 1
    for _, sample_task in sample_tasks[nxt:]:
        sample_task()

    for h in range(HEADS_PER_GROUP):
        lses = [heads[g, h][1] for g in range(N_GROUPS)]
        m = jnp.maximum(jnp.maximum(lses[0], lses[1]), lses[2])
        es = [jnp.exp(lse - m) for lse in lses]
        inv = 1.0 / (es[0] + es[1] + es[2])
        for g in range(N_GROUPS):
            base = g * GROUP_W + h * HEAD_DIM
            attnt_ref[base:base + HEAD_DIM, :] += (heads[g, h][0] * (es[g] * inv)) * onehot


def _attention(qkvt, a_blocks, caches, half_shifted, weights, stats, blocks_per_seq):
    n_req = qkvt.shape[1]
    n_blocks = a_blocks[0].shape[0]
    assert n_blocks % n_req == 0, "query blocks are dealt evenly over the per-request grid steps"
    tbb = n_blocks // n_req
    nq = KEYS_PER_QUERY_BLOCK
    a_specs, operands = [], [qkvt]
    for a in a_blocks:
        a_specs += [pl.BlockSpec((tbb, nq, 3 * GROUP_W), lambda r: (r, 0, 0)),
                    pl.BlockSpec((1, nq, 3 * GROUP_W), lambda r: (jnp.maximum(r * tbb - 1, 0), 0, 0))]
        operands += [a, a]
    v_specs = [pl.BlockSpec((None, HEADS_PER_GROUP) + c.shape[2:], lambda r: (r, 1, 0, 0)) for c in caches]
    w_specs = [pl.BlockSpec((None,) + w.shape[1:], lambda r: (r, 0, 0)) for w in weights]
    stats_spec = pl.BlockSpec((None,) + stats.shape[1:], lambda r: (r, 0, 0))
    og_spec = pl.BlockSpec((tbb, nq, GROUP_W), lambda r: (r, 0, 0))
    operands += [*caches, *weights, stats, *half_shifted]
    first_aliased = len(operands) - len(half_shifted)
    return pl.pallas_call(
        functools.partial(_attn_kernel, blocks_per_seq=blocks_per_seq),
        grid=(n_req,),
        in_specs=[_resident(qkvt.shape)] + a_specs + v_specs + w_specs + [stats_spec]
        + [pl.BlockSpec(memory_space=pl.ANY)] * len(half_shifted),
        out_specs=[og_spec] * (2 * N_GROUPS) + v_specs + [pl.BlockSpec((D_ATTN, n_req), lambda r: (0, 0))],
        out_shape=[jax.ShapeDtypeStruct((n_blocks, nq, GROUP_W), BF16)] * N_GROUPS
        + [jax.ShapeDtypeStruct((n_blocks, nq, GROUP_W), F32)] * N_GROUPS
        + [jax.ShapeDtypeStruct(c.shape, F32) for c in caches]
        + [jax.ShapeDtypeStruct((D_ATTN, n_req), F32)],
        input_output_aliases={first_aliased + g: 2 * N_GROUPS + g for g in range(len(half_shifted))},
        compiler_params=pltpu.CompilerParams(dimension_semantics=("arbitrary",),
                                             vmem_limit_bytes=VMEM_LIMIT_BYTES),
        name="k_attn",
    )(*operands)


def _merge_groups(o_and_lse):
    m = o_and_lse[0][1]
    for _, lse in o_and_lse[1:]:
        m = jnp.maximum(m, lse)
    es = [jnp.exp(lse - m) for _, lse in o_and_lse]
    inv = 1.0 / (es[0] + es[1] + es[2])
    return jnp.concatenate([(o * (e * inv)).astype(BF16) for (o, _), e in zip(o_and_lse, es)], axis=1)


def _mix_tail(x1, gg, u, mixed, attn, mod_ref, ng_ref, wba_ref, wbb_ref, wo_ref,
              wu_ref, wd_ref, fg_ref, hb_scr, acc_scr):
    gm = (u * mixed).astype(BF16)
    merged = (jax.nn.sigmoid(gg[:, :D_MODEL]) * jnp.dot(attn, wba_ref[...], preferred_element_type=F32)
              + jax.nn.sigmoid(gg[:, D_MODEL:]) * jnp.dot(gm, wbb_ref[...], preferred_element_type=F32))
    y = jnp.dot(merged.astype(BF16), wo_ref[...], preferred_element_type=F32)
    x2 = x1 + _mod(mod_ref, 5) * y
    hb_scr[...] = _pre(x2, ng_ref[2:3, :], mod_ref, 2).astype(BF16)
    f = _swiglu(hb_scr, wu_ref, wd_ref, acc_scr)
    x3 = x2 + (0.5 * _mod(mod_ref, 8)) * f
    return _rms(x3) * fg_ref[...]


def _layer_norm_rows(v, g_ref, b_ref):
    mu = jnp.mean(v, axis=-1, keepdims=True)
    var = jnp.mean(jnp.square(v - mu), axis=-1, keepdims=True)
    return ((v - mu) * lax.rsqrt(var + EPS)) * g_ref[...] + b_ref[...]


def _mix_ffn_prompt_kernel(x1_ref, ob0_ref, ob1_ref, ob2_ref, ls0_ref, ls1_ref, ls2_ref, mod_ref, ng_ref, wr_ref,
                           lng_ref, lnb_ref,
                           ws_ref, bst_ref, wba_ref, wbb_ref, wo_ref, wu_ref, wd_ref, fg_ref,
                           y_ref, vrows_ref, hb_scr, acc_scr, un_scr, mixed_scr):
    tm = x1_ref.shape[0]
    x1 = x1_ref[...]
    h1 = _pre(x1, ng_ref[1:2, :], mod_ref, 1).astype(BF16)
    uv = jnp.dot(h1, wr_ref[:, :2 * D_GMLP], preferred_element_type=F32)
    gg = jnp.dot(h1, wr_ref[:, 2 * D_GMLP:], preferred_element_type=F32)
    u = uv[:, :D_GMLP]
    vn = _layer_norm_rows(uv[:, D_GMLP:], lng_ref, lnb_ref)

    vrows_ref[...] = vn[tm - CHUNK:, :]

    vnb = vn.astype(BF16)
    tri = (lax.broadcasted_iota(jnp.int32, (CHUNK, CHUNK), 0)
           >= lax.broadcasted_iota(jnp.int32, (CHUNK, CHUNK), 1))
    gw = D_GMLP // GMLP_GROUPS
    for g in range(GMLP_GROUPS):
        ws = jnp.where(tri, ws_ref[g], 0.0).astype(BF16)
        bias = bst_ref[:, g:g + 1]
        for c in range(tm // CHUNK):
            mixed_scr[c * CHUNK:(c + 1) * CHUNK, g * gw:(g + 1) * gw] = (
                jnp.dot(ws, vnb[c * CHUNK:(c + 1) * CHUNK, g * gw:(g + 1) * gw],
                        preferred_element_type=F32) + bias)

    half = GROUP_W // LANES
    for g, (ob_ref, lse_ref) in enumerate(((ob0_ref, ls0_ref), (ob1_ref, ls1_ref), (ob2_ref, ls2_ref))):
        dil = DIL_GROUPS[g][1]
        n = tm // dil
        for r in range(dil):
            for c in range(half):
                lanes = slice(c * LANES, (c + 1) * LANES)
                un_scr[g, c, pl.ds(r, n, stride=dil), :] = ob_ref[r, :, lanes].astype(F32)
                un_scr[g, half + c, pl.ds(r, n, stride=dil), :] = lse_ref[r, :, lanes]
    o_and_lse = []
    for g in range(N_GROUPS):
        cols = [un_scr[g, c] for c in range(2 * half)]
        o_and_lse.append((jnp.concatenate(cols[:half], axis=1), jnp.concatenate(cols[half:], axis=1)))
    attn = _merge_groups(o_and_lse)

    y_ref[...] = _mix_tail(x1, gg, u, mixed_scr[...], attn, mod_ref, ng_ref, wba_ref, wbb_ref, wo_ref,
                           wu_ref, wd_ref, fg_ref, hb_scr, acc_scr)


def _mix_ffn_prompt(x1, obs, lses, mod, norm_g, wr, lng, lnb, ws, bst, wba, wbb, wo, wu, wd, fg):
    bsz, seq_len, _ = x1.shape
    tm = ROW_TILE
    weights = (norm_g, wr, lng, lnb, ws, bst, wba, wbb, wo, wu, wd, fg)
    row_spec = pl.BlockSpec((None, tm, D_MODEL), lambda b, i: (b, i, 0))
    og_specs = [pl.BlockSpec((None, dil, tm // dil, GROUP_W), lambda b, i: (b, 0, i, 0))
                for _, dil in DIL_GROUPS] * 2
    return pl.pallas_call(
        _mix_ffn_prompt_kernel,
        grid=(bsz, seq_len // tm),
        in_specs=[row_spec] + og_specs + [pl.BlockSpec((None,) + mod.shape[1:], lambda b, i: (b, 0, 0))]
        + [_resident(w.shape) for w in weights],
        out_specs=[row_spec, pl.BlockSpec((None, CHUNK, D_GMLP), lambda b, i: (b, 0, 0))],
        out_shape=[jax.ShapeDtypeStruct(x1.shape, F32), jax.ShapeDtypeStruct((bsz, CHUNK, D_GMLP), F32)],
        scratch_shapes=[pltpu.VMEM((tm, D_MODEL), BF16), pltpu.VMEM((tm, D_MODEL), F32),
                        pltpu.VMEM((N_GROUPS, 2 * GROUP_W // LANES, tm, LANES), F32),
                        pltpu.VMEM((tm, D_GMLP), F32)],
        compiler_params=pltpu.CompilerParams(dimension_semantics=("arbitrary", "arbitrary"),
                                             vmem_limit_bytes=VMEM_LIMIT_BYTES),
        name="k_mix_ffn",
    )(x1, *obs, *lses, mod, *weights)


def _mix_ffn_sample_kernel(x1_ref, attnt_ref, mod_ref, ng_ref, wr_ref, lng_ref, lnb_ref,
                           ws0_ref, bs0_ref, wba_ref, wbb_ref, wo_ref, wu_ref, wd_ref, fg_ref,
                           y_ref, vrows_ref, hb_scr, acc_scr):
    x1 = x1_ref[...]
    h1 = _pre(x1, ng_ref[1:2, :], mod_ref, 1).astype(BF16)
    uv = jnp.dot(h1, wr_ref[:, :2 * D_GMLP], preferred_element_type=F32)
    gg = jnp.dot(h1, wr_ref[:, 2 * D_GMLP:], preferred_element_type=F32)
    u = uv[:, :D_GMLP]
    vn = _layer_norm_rows(uv[:, D_GMLP:], lng_ref, lnb_ref)
    vrows_ref[...] = vn
    mixed = ws0_ref[...] * vn + bs0_ref[...]
    attn = attnt_ref[...].T.astype(BF16)
    y_ref[...] = _mix_tail(x1, gg, u, mixed, attn, mod_ref, ng_ref, wba_ref, wbb_ref, wo_ref,
                           wu_ref, wd_ref, fg_ref, hb_scr, acc_scr)


def _mix_ffn_sample(x1, attnt, mod, norm_g, wr, lng, lnb, ws0, bs0, wba, wbb, wo, wu, wd, fg):
    n = x1.shape[0]
    operands = (x1, attnt, mod, norm_g, wr, lng, lnb, ws0, bs0, wba, wbb, wo, wu, wd, fg)
    return pl.pallas_call(
        _mix_ffn_sample_kernel,
        grid=(1,),
        in_specs=[_resident(w.shape) for w in operands],
        out_specs=[_whole(x1.shape), _whole((n, D_GMLP))],
        out_shape=[jax.ShapeDtypeStruct(x1.shape, F32), jax.ShapeDtypeStruct((n, D_GMLP), F32)],
        scratch_shapes=[pltpu.VMEM((n, D_MODEL), BF16), pltpu.VMEM((n, D_MODEL), F32)],
        compiler_params=pltpu.CompilerParams(dimension_semantics=("arbitrary",),
                                             vmem_limit_bytes=VMEM_LIMIT_BYTES),
        name="k_mix_ffn_sample",
    )(*operands)


def _to_buffer_layout(cache):
    n, rows = cache.shape[0], cache.shape[1]
    return cache.transpose(0, 2, 3, 4, 1).reshape(n, 2 * HEADS_PER_GROUP, HEAD_DIM, rows)


def _from_buffer_layout(buf):
    n, rows = buf.shape[0], buf.shape[-1]
    return buf.reshape(n, 2, HEADS_PER_GROUP, HEAD_DIM, rows).transpose(0, 4, 1, 2, 3)[None]


def kernel(x_prompt, x_sample, c_prompt, c_sample, cache_kv_g0, cache_kv_g1, cache_kv_g2, ada_w, ada_b, norm_g,
           ffn1_up, ffn1_down, w_in, w_branch_a, w_branch_b, w_out, v_ln_g, v_ln_b, w_spatial, b_spatial,
           ffn2_up, ffn2_down, final_g):
    assert ada_w.shape[0] == 1, "one layer"
    bsz, seq_len, _ = x_prompt.shape
    n_req = x_sample.shape[0]

    wu1, wd1 = ffn1_up[0].astype(BF16), ffn1_down[0].astype(BF16)
    wu2, wd2 = ffn2_up[0].astype(BF16), ffn2_down[0].astype(BF16)
    wqkv = w_in[0][:, :3 * D_ATTN].astype(BF16)
    wr = w_in[0][:, 3 * D_ATTN:].astype(BF16)
    wkvt = wqkv[:, D_ATTN:].T
    wba, wbb, wo = w_branch_a[0].astype(BF16), w_branch_b[0].astype(BF16), w_out[0].astype(BF16)
    ng = norm_g[0]
    lng, lnb = v_ln_g, v_ln_b
    ws = w_spatial[0]
    bst = b_spatial[0].T
    gw = D_GMLP // GMLP_GROUPS
    ws0 = jnp.repeat(ws[:, 0, 0], gw)[None, :]
    bs0 = jnp.repeat(b_spatial[0][:, 0], gw)[None, :]
    fg = final_g[None, :]

    mods = _ada_table(jnp.concatenate([c_prompt, c_sample], axis=0), ada_w[0], ada_b)
    mod_p = mods[:, :bsz].transpose(1, 0, 2)
    mod_s = mods[:, bsz:]

    x1s, qkvt = _ffn_qkv_sample(x_sample.reshape(n_req, D_MODEL), mod_s, ng, wu1, wd1, wqkv)
    caches = [_to_buffer_layout(c[0]) for c in (cache_kv_g0, cache_kv_g1, cache_kv_g2)]
    (x1, a0, a1, a2, kv0, kv1, kv2, nk0, nk1, nk2, pn0, pn1, pn2, stats) = _ffn_qkv_prompt(
        x_prompt, mod_p, ng, wu1, wd1, wqkv, wkvt, qkvt, caches)

    nq = KEYS_PER_QUERY_BLOCK
    a_blocks = [a.reshape(-1, nq, 3 * GROUP_W) for a in (a0, a1, a2)]
    blocks_per_seq = tuple(seq_len // dil // nq for _, dil in DIL_GROUPS)
    ob0, ob1, ob2, ls0, ls1, ls2, s0, s1, s2, attnt = _attention(
        qkvt, a_blocks, caches, (nk0, nk1, nk2), (pn0, pn1, pn2), stats, blocks_per_seq)
    per_residue = lambda og, dil: og.reshape(bsz, dil, seq_len // dil, GROUP_W)
    obs = [per_residue(og, dil) for og, (_, dil) in zip((ob0, ob1, ob2), DIL_GROUPS)]
    lses = [per_residue(og, dil) for og, (_, dil) in zip((ls0, ls1, ls2), DIL_GROUPS)]

    y_prompt, vrows_p = _mix_ffn_prompt(x1, obs, lses, mod_p, ng, wr, lng, lnb, ws, bst, wba, wbb, wo,
                                        wu2, wd2, fg)
    y_sample, vrows_s = _mix_ffn_sample(x1s, attnt, mod_s, ng, wr, lng, lnb, ws0, bs0, wba, wbb, wo,
                                        wu2, wd2, fg)

    return (y_prompt, y_sample.reshape(n_req, 1, D_MODEL),
            _from_buffer_layout(kv0), _from_buffer_layout(kv1), _from_buffer_layout(kv2), vrows_p[None],
            _from_buffer_layout(s0), _from_buffer_layout(s1), _from_buffer_layout(s2),
            vrows_s.reshape(1, n_req, 1, D_GMLP))
```

```python
import functools

import jax
import jax.numpy as jnp
from jax import lax
from jax.experimental import pallas as pl
from jax.experimental.pallas import tpu as pltpu

F32 = jnp.float32
BF16 = jnp.bfloat16

D_MODEL = 1024
HEAD_DIM = 64
HEADS_PER_GROUP = 4
HEADS_PER_PRODUCT = 2
GROUP_W = HEADS_PER_GROUP * HEAD_DIM
DIL_GROUPS = ((128, 1), (512, 4), (2048, 16))
N_GROUPS = len(DIL_GROUPS)
D_ATTN = N_GROUPS * GROUP_W
D_GMLP = 512
GMLP_GROUPS = 4
CHUNK = 128
D_FF = 2816
FF_CHUNK = 256
N_FF_CHUNKS = D_FF // FF_CHUNK
EPS = 1e-6
QK_SCALE = HEAD_DIM ** -0.5
KEYS_PER_QUERY_BLOCK = 128
MASKED = -1e30
LANES = 128

VMEM_LIMIT_BYTES = 60 * 1024 * 1024
ROW_TILE = 512


def _resident(shape):
    zeros = (0,) * len(shape)
    return pl.BlockSpec(shape, lambda *_: zeros, pipeline_mode=pl.Buffered(1))


def _whole(shape):
    zeros = (0,) * len(shape)
    return pl.BlockSpec(shape, lambda *_: zeros)


def _sigmoid(x):
    return 0.5 * jnp.tanh(0.5 * x) + 0.5


def _rms(x):
    return x * lax.rsqrt(jnp.mean(x * x, axis=-1, keepdims=True) + EPS)


def _mod(mod_ref, i):
    if len(mod_ref.shape) == 2:
        return mod_ref[i:i + 1, :]
    return mod_ref[i]


def _pre(x, g, mod_ref, sub):
    return (_rms(x) * g) * (1.0 + _mod(mod_ref, 3 * sub + 1)) + _mod(mod_ref, 3 * sub)


def _swiglu_chunk(c, h_ref, wu_ref, wd_ref, acc_ref):
    lo, hi = c * FF_CHUNK, (c + 1) * FF_CHUNK
    h = h_ref[...]
    a = jnp.dot(h, wu_ref[:, lo:hi], preferred_element_type=F32)
    b = jnp.dot(h, wu_ref[:, D_FF + lo:D_FF + hi], preferred_element_type=F32)
    act = ((a * _sigmoid(a)) * b).astype(BF16)
    part = jnp.dot(act, wd_ref[lo:hi, :], preferred_element_type=F32)
    if c == 0:
        acc_ref[...] = part
    else:
        acc_ref[...] += part


def _swiglu(h_ref, wu_ref, wd_ref, acc_ref, side_tasks=()):
    dealt = 0
    for c in range(N_FF_CHUNKS):
        _swiglu_chunk(c, h_ref, wu_ref, wd_ref, acc_ref)
        while dealt * N_FF_CHUNKS < len(side_tasks) * (c + 1):
            side_tasks[dealt]()
            dealt += 1
    return acc_ref[...]


def _ada_kernel(c_ref, w_ref, b_ref, o_ref):
    c = c_ref[...]
    h = (c * _sigmoid(c)).astype(BF16)
    o_ref[...] = jnp.dot(h, w_ref[...].astype(BF16), preferred_element_type=F32) + b_ref[...]


def _ada_table(c_all, ada_w, ada_b):
    n = c_all.shape[0]
    n_ada = ada_w.shape[1] // D_MODEL
    return pl.pallas_call(
        _ada_kernel,
        grid=(n_ada,),
        in_specs=[
            pl.BlockSpec((n, D_MODEL), lambda i: (0, 0)),
            pl.BlockSpec((D_MODEL, D_MODEL), lambda i: (0, i)),
            pl.BlockSpec((1, D_MODEL), lambda i: (0, i)),
        ],
        out_specs=pl.BlockSpec((None, n, D_MODEL), lambda i: (i, 0, 0)),
        out_shape=jax.ShapeDtypeStruct((n_ada, n, D_MODEL), F32),
        compiler_params=pltpu.CompilerParams(dimension_semantics=("arbitrary",)),
        name="k_ada",
    )(c_all, ada_w, ada_b)


def _ffn1_front(x_ref, mod_ref, ng_ref, wu_ref, wd_ref, hb_scr, acc_scr, side_tasks=()):
    x = x_ref[...]
    hb_scr[...] = _pre(x, ng_ref[0:1, :], mod_ref, 0).astype(BF16)
    f = _swiglu(hb_scr, wu_ref, wd_ref, acc_scr, side_tasks)
    x1 = x + (0.5 * _mod(mod_ref, 2)) * f
    return x1, _pre(x1, ng_ref[1:2, :], mod_ref, 1)


def _ffn_qkv_prompt_kernel(x_ref, mod_ref, ng_ref, wu_ref, wd_ref, wqkv_ref, wkvt_ref,
                           qkvt_ref, ck0_ref, ck1_ref, ck2_ref,
                           x1_ref, a0_ref, a1_ref, a2_ref, kv0_ref, kv1_ref, kv2_ref,
                           nk0_ref, nk1_ref, nk2_ref, pn0_ref, pn1_ref, pn2_ref, stats_ref,
                           hb_scr, acc_scr, h1_scr, hp_scr, *, seq_len):
    tm = x_ref.shape[0]
    i = pl.program_id(1)

    request = pl.program_id(0) * pl.num_programs(1) + i
    onehot = (lax.broadcasted_iota(jnp.int32, (1, qkvt_ref.shape[1]), 1) == request).astype(F32)
    cols = jnp.sum(qkvt_ref[0:2 * D_ATTN, :] * onehot, axis=1, keepdims=True)

    def scores_task(g, h, c_ref, n_ref, pn_ref):
        p, p_new, l, lse = _sample_scores(g, h, cols, c_ref, n_ref)
        inv_l = 1.0 / l
        pn_ref[h:h + 1, :] = p * inv_l
        row = g * HEADS_PER_GROUP + h
        stats_ref[row:row + 1, :] = jnp.broadcast_to(p_new * inv_l, (1, LANES))
        n_heads = N_GROUPS * HEADS_PER_GROUP
        stats_ref[n_heads + row:n_heads + row + 1, :] = jnp.broadcast_to(lse, (1, LANES))

    side_tasks = [functools.partial(scores_task, g, h, c_ref, n_ref, pn_ref)
                  for g, (c_ref, n_ref, pn_ref) in enumerate(((ck0_ref, nk0_ref, pn0_ref), (ck1_ref, nk1_ref, pn1_ref),
                                                              (ck2_ref, nk2_ref, pn2_ref)))
                  for h in range(HEADS_PER_GROUP)]
    x1, h1 = _ffn1_front(x_ref, mod_ref, ng_ref, wu_ref, wd_ref, hb_scr, acc_scr, side_tasks)
    x1_ref[...] = x1
    for c in range(D_MODEL // LANES):
        h1_scr[c] = h1[:, c * LANES:(c + 1) * LANES]
    hb_scr[...] = h1.astype(BF16)

    for g, a_ref in enumerate((a0_ref, a1_ref, a2_ref)):
        dil = DIL_GROUPS[g][1]
        n = tm // dil
        if dil == 1:
            hp = hb_scr[...]
        else:
            for r in range(dil):
                rows = [h1_scr[c, pl.ds(r, n, stride=dil), :] for c in range(D_MODEL // LANES)]
                hp_scr[r * n:(r + 1) * n, :] = jnp.concatenate(rows, axis=1).astype(BF16)
            hp = hp_scr[...]
        for part in range(3):
            lo = part * D_ATTN + g * GROUP_W
            res = jnp.dot(hp, wqkv_ref[:, lo:lo + GROUP_W], preferred_element_type=F32)
            if part == 0:
                res = res * QK_SCALE
            a_ref[:, :, part * GROUP_W:(part + 1) * GROUP_W] = res.astype(BF16).reshape(dil, n, GROUP_W)

    for g, kv_ref in enumerate((kv0_ref, kv1_ref, kv2_ref)):
        window = DIL_GROUPS[g][0]
        first_tile = (seq_len - window) // tm

        @pl.when(i >= first_tile)
        def _(g=g, kv_ref=kv_ref, window=window):
            for part in range(2):
                lo = part * D_ATTN + g * GROUP_W
                t = lax.dot_general(wkvt_ref[lo:lo + GROUP_W, :], hb_scr[...], (((1,), (1,)), ((), ())),
                                    preferred_element_type=F32)
                if window < tm:
                    t = t[:, tm - window:]
                kv_ref[part * HEADS_PER_GROUP:(part + 1) * HEADS_PER_GROUP] = t.reshape(
                    HEADS_PER_GROUP, HEAD_DIM, t.shape[-1])


def _ffn_qkv_prompt(x, mod, norm_g, wu, wd, wqkv, wkvt, qkvt, caches):
    bsz, seq_len, _ = x.shape
    tm = ROW_TILE
    nt = seq_len // tm
    n_req = qkvt.shape[1]
    assert bsz * nt == n_req, "one sample request rides on each row-tile grid step"
    weights = (norm_g, wu, wd, wqkv, wkvt, qkvt)
    n_heads = N_GROUPS * HEADS_PER_GROUP
    k_half = lambda c: pl.BlockSpec((None, HEADS_PER_GROUP) + c.shape[2:], lambda b, i: (b * nt + i, 0, 0, 0))
    side_in_specs = [k_half(c) for c in caches]
    side_out_specs = ([k_half(c) for c in caches]
                      + [pl.BlockSpec((None, HEADS_PER_GROUP, c.shape[-1]), lambda b, i: (b * nt + i, 0, 0))
                         for c in caches]
                      + [pl.BlockSpec((None, 2 * n_heads, LANES), lambda b, i: (b * nt + i, 0, 0))])
    side_out_shapes = ([jax.ShapeDtypeStruct(c.shape, F32) for c in caches]
                       + [jax.ShapeDtypeStruct((n_req, HEADS_PER_GROUP, c.shape[-1]), F32) for c in caches]
                       + [jax.ShapeDtypeStruct((n_req, 2 * n_heads, LANES), F32)])
    a_shapes, a_specs, kv_shapes, kv_specs = [], [], [], []
    for window, dil in DIL_GROUPS:
        a_shapes.append(jax.ShapeDtypeStruct((bsz, dil, seq_len // dil, 3 * GROUP_W), BF16))
        a_specs.append(pl.BlockSpec((None, dil, tm // dil, 3 * GROUP_W), lambda b, i: (b, 0, i, 0)))
        first_tile = (seq_len - window) // tm
        kv_shapes.append(jax.ShapeDtypeStruct((bsz, 2 * HEADS_PER_GROUP, HEAD_DIM, window), F32))
        kv_specs.append(pl.BlockSpec(
            (None, 2 * HEADS_PER_GROUP, HEAD_DIM, min(window, tm)),
            lambda b, i, first_tile=first_tile: (b, 0, 0, jnp.maximum(i - first_tile, 0))))
    row_spec = pl.BlockSpec((None, tm, D_MODEL), lambda b, i: (b, i, 0))
    return pl.pallas_call(
        functools.partial(_ffn_qkv_prompt_kernel, seq_len=seq_len),
        grid=(bsz, seq_len // tm),
        in_specs=[row_spec, pl.BlockSpec((None,) + mod.shape[1:], lambda b, i: (b, 0, 0))]
        + [_resident(w.shape) for w in weights] + side_in_specs,
        out_specs=[row_spec] + a_specs + kv_specs + side_out_specs,
        out_shape=[jax.ShapeDtypeStruct(x.shape, F32)] + a_shapes + kv_shapes + side_out_shapes,
        scratch_shapes=[pltpu.VMEM((tm, D_MODEL), BF16), pltpu.VMEM((tm, D_MODEL), F32),
                        pltpu.VMEM((D_MODEL // LANES, tm, LANES), F32), pltpu.VMEM((tm, D_MODEL), BF16)],
        compiler_params=pltpu.CompilerParams(dimension_semantics=("arbitrary", "arbitrary"),
                                             vmem_limit_bytes=VMEM_LIMIT_BYTES),
        name="k_ffn_qkv",
    )(x, mod, *weights, *caches)


def _ffn_qkv_sample_kernel(x_ref, mod_ref, ng_ref, wu_ref, wd_ref, wqkv_ref,
                           x1_ref, qkvt_ref, hb_scr, acc_scr):
    x1, h1 = _ffn1_front(x_ref, mod_ref, ng_ref, wu_ref, wd_ref, hb_scr, acc_scr)
    x1_ref[...] = x1
    qkv = jnp.dot(h1.astype(BF16), wqkv_ref[...], preferred_element_type=F32)
    col = lax.broadcasted_iota(jnp.int32, (1, 3 * D_ATTN), 1)
    qkv = qkv * jnp.where(col < D_ATTN, QK_SCALE, 1.0)
    qkvt_ref[...] = qkv.T


def _ffn_qkv_sample(x, mod, norm_g, wu, wd, wqkv):
    n = x.shape[0]
    weights = (norm_g, wu, wd, wqkv)
    return pl.pallas_call(
        _ffn_qkv_sample_kernel,
        grid=(1,),
        in_specs=[_resident(x.shape), _resident(mod.shape)] + [_resident(w.shape) for w in weights],
        out_specs=[_whole(x.shape), _whole((3 * D_ATTN, n))],
        out_shape=[jax.ShapeDtypeStruct(x.shape, F32), jax.ShapeDtypeStruct((3 * D_ATTN, n), F32)],
        scratch_shapes=[pltpu.VMEM((n, D_MODEL), BF16), pltpu.VMEM((n, D_MODEL), F32)],
        compiler_params=pltpu.CompilerParams(dimension_semantics=("arbitrary",),
                                             vmem_limit_bytes=VMEM_LIMIT_BYTES),
        name="k_ffn_qkv_sample",
    )(x, mod, *weights)


def _band_scores(q, k2, lane_head):
    pairs = []
    for first in range(0, HEADS_PER_GROUP, HEADS_PER_PRODUCT):
        qs = jnp.concatenate([jnp.where(lane_head == h, q, jnp.zeros_like(q))
                              for h in range(first, first + HEADS_PER_PRODUCT)], axis=0)
        pairs.append(lax.dot_general(qs, k2, (((1,), (1,)), ((), ())), preferred_element_type=F32))
    return pairs


def _band_softmax_values(score_pairs, v2, first_key, band_key, lane_head, ones):
    nq = KEYS_PER_QUERY_BLOCK
    o = lse_b = None
    for pair, s in enumerate(score_pairs):
        s = jnp.where(band_key >= first_key, s, MASKED)
        m = jnp.max(s, axis=1, keepdims=True)
        p = jnp.exp(s - m).astype(BF16)
        r = jnp.dot(p, v2, preferred_element_type=F32)
        l = jnp.dot(p, ones, preferred_element_type=F32)
        inv_l = 1.0 / l
        lse = m + jnp.log(l)
        for j in range(HEADS_PER_PRODUCT):
            h = pair * HEADS_PER_PRODUCT + j
            rows = slice(j * nq, (j + 1) * nq)
            o_h = r[rows] * jnp.concatenate([inv_l[rows]] * (GROUP_W // LANES), axis=1)
            lse_h = jnp.concatenate([lse[rows]] * (GROUP_W // LANES), axis=1)
            o = o_h if h == 0 else jnp.where(lane_head == h, o_h, o)
            lse_b = lse_h if h == 0 else jnp.where(lane_head == h, lse_h, lse_b)
    return o, lse_b


def _shifted(buf, newest, new_col):
    return jnp.where(newest, new_col, pltpu.roll(buf, buf.shape[-1] - 1, axis=1))


def _sample_scores(g, h, cols, c_ref, n_ref):
    window, dil = DIL_GROUPS[g]
    rows = c_ref.shape[-1]
    col = lax.broadcasted_iota(jnp.int32, (1, rows), 1)
    dist = rows - col
    strided = (dist % dil == 0) & (dist <= window)
    base = g * GROUP_W + h * HEAD_DIM
    qc = cols[base:base + HEAD_DIM]
    kc = cols[D_ATTN + base:D_ATTN + base + HEAD_DIM]
    kt = c_ref[h]
    s = jnp.where(strided, jnp.sum(kt * qc, axis=0, keepdims=True), MASKED)
    s_new = jnp.sum(kc * qc, axis=0, keepdims=True)
    m = jnp.maximum(jnp.max(s, axis=1, keepdims=True), s_new)
    p = jnp.exp(s - m)
    p_new = jnp.exp(s_new - m)
    l = jnp.sum(p, axis=1, keepdims=True) + p_new
    n_ref[h] = _shifted(kt, col == rows - 1, kc)
    return p, p_new, l, m + jnp.log(l)


def _sample_values(g, h, v_cols, c_ref, n_ref, pn, pn_new):
    rows = c_ref.shape[-1]
    col = lax.broadcasted_iota(jnp.int32, (1, rows), 1)
    base = g * GROUP_W + h * HEAD_DIM
    vc = v_cols[base:base + HEAD_DIM]
    vt = c_ref[h]
    n_ref[h] = _shifted(vt, col == rows - 1, vc)
    return jnp.sum(vt * pn, axis=1, keepdims=True) + vc * pn_new


def _attn_kernel(qkvt_ref, a0_ref, p0_ref, a1_ref, p1_ref, a2_ref, p2_ref, c0_ref, c1_ref, c2_ref,
                 pn0_ref, pn1_ref, pn2_ref, stats_ref, nk0_ref, nk1_ref, nk2_ref,
                 ob0_ref, ob1_ref, ob2_ref, ls0_ref, ls1_ref, ls2_ref, n0_ref, n1_ref, n2_ref, attnt_ref,
                 *, blocks_per_seq):
    del nk0_ref, nk1_ref, nk2_ref
    step = pl.program_id(0)
    nq = KEYS_PER_QUERY_BLOCK

    @pl.when(step == 0)
    def _():
        attnt_ref[...] = jnp.zeros_like(attnt_ref)

    lane_head = lax.broadcasted_iota(jnp.int32, (nq, GROUP_W), 1) // HEAD_DIM
    qi = lax.broadcasted_iota(jnp.int32, (HEADS_PER_PRODUCT * nq, 2 * nq), 0) % nq
    kj = lax.broadcasted_iota(jnp.int32, (HEADS_PER_PRODUCT * nq, 2 * nq), 1)
    band_key = jnp.where((kj >= qi) & (kj <= qi + nq), kj, -1)
    ones = jnp.ones((2 * nq, LANES), BF16)

    scores = {}

    def scores_of(g, t, a_ref, p_ref, og_ref):
        before = p_ref[0] if t == 0 else a_ref[t - 1]
        own = a_ref[t]
        k2 = jnp.concatenate([before[:, GROUP_W:2 * GROUP_W], own[:, GROUP_W:2 * GROUP_W]], axis=0)
        scores[g, t] = _band_scores(own[:, 0:GROUP_W], k2, lane_head)

    def band_task(g, t, a_ref, p_ref, og_ref):
        tbb = a_ref.shape[0]
        before = p_ref[0] if t == 0 else a_ref[t - 1]
        v2 = jnp.concatenate([before[:, 2 * GROUP_W:], a_ref[t][:, 2 * GROUP_W:]], axis=0)
        first_key = jnp.where(((step * tbb + t) % blocks_per_seq[g]) == 0, nq, 0)
        o, lse_b = _band_softmax_values(scores.pop((g, t)), v2, first_key, band_key, lane_head, ones)
        ob_ref, ls_ref = og_ref
        ob_ref[t] = o.astype(BF16)
        ls_ref[t] = lse_b

    blocks = [(g, t, a_ref, p_ref, og_ref)
              for g, (a_ref, p_ref, og_ref) in enumerate(((a0_ref, p0_ref, (ob0_ref, ls0_ref)),
                                                          (a1_ref, p1_ref, (ob1_ref, ls1_ref)),
                                                          (a2_ref, p2_ref, (ob2_ref, ls2_ref))))
              for t in range(a_ref.shape[0])]
    score_ahead = 2
    for blk in blocks[:score_ahead]:
        scores_of(*blk)

    def make_band_task(i):
        def task():
            band_task(*blocks[i])
            if i + score_ahead < len(blocks):
                scores_of(*blocks[i + score_ahead])
        return task

    band_tasks = [make_band_task(i) for i in range(len(blocks))]

    n_req = qkvt_ref.shape[1]
    onehot = (lax.broadcasted_iota(jnp.int32, (1, n_req), 1) == step).astype(F32)
    v_cols = jnp.sum(qkvt_ref[2 * D_ATTN:, :] * onehot, axis=1, keepdims=True)
    n_heads = N_GROUPS * HEADS_PER_GROUP
    heads = {}

    def values_task(g, h, c_ref, n_ref, pn_ref):
        row = g * HEADS_PER_GROUP + h
        pn_new = stats_ref[row:row + 1, 0:1]
        lse = stats_ref[n_heads + row:n_heads + row + 1, 0:1]
        heads[g, h] = (_sample_values(g, h, v_cols, c_ref, n_ref, pn_ref[h:h + 1, :], pn_new), lse)

    sample_tasks = []
    for g, (c_ref, n_ref, pn_ref) in enumerate(((c0_ref, n0_ref, pn0_ref), (c1_ref, n1_ref, pn1_ref),
                                                (c2_ref, n2_ref, pn2_ref))):
        for h in range(HEADS_PER_GROUP):
            sample_tasks.append((c_ref.shape[-1], functools.partial(values_task, g, h, c_ref, n_ref, pn_ref)))

    total = sum(cost for cost, _ in sample_tasks)
    dealt = nxt = 0
    for i, task in enumerate(band_tasks):
        task()
        while nxt < len(sample_tasks) and dealt * len(band_tasks) < total * (i + 1):
            cost, sample_task = sample_tasks[nxt]
            sample_task()
            dealt += cost
            nxt += 1
    for _, sample_task in sample_tasks[nxt:]:
        sample_task()

    for h in range(HEADS_PER_GROUP):
        lses = [heads[g, h][1] for g in range(N_GROUPS)]
        m = jnp.maximum(jnp.maximum(lses[0], lses[1]), lses[2])
        es = [jnp.exp(lse - m) for lse in lses]
        inv = 1.0 / (es[0] + es[1] + es[2])
        for g in range(N_GROUPS):
            base = g * GROUP_W + h * HEAD_DIM
            attnt_ref[base:base + HEAD_DIM, :] += (heads[g, h][0] * (es[g] * inv)) * onehot


def _attention(qkvt, a_blocks, caches, half_shifted, weights, stats, blocks_per_seq):
    n_req = qkvt.shape[1]
    n_blocks = a_blocks[0].shape[0]
    assert n_blocks % n_req == 0, "query blocks are dealt evenly over the per-request grid steps"
    tbb = n_blocks // n_req
    nq = KEYS_PER_QUERY_BLOCK
    a_specs, operands = [], [qkvt]
    for a in a_blocks:
        a_specs += [pl.BlockSpec((tbb, nq, 3 * GROUP_W), lambda r: (r, 0, 0)),
                    pl.BlockSpec((1, nq, 3 * GROUP_W), lambda r: (jnp.maximum(r * tbb - 1, 0), 0, 0))]
        operands += [a, a]
    v_specs = [pl.BlockSpec((None, HEADS_PER_GROUP) + c.shape[2:], lambda r: (r, 1, 0, 0)) for c in caches]
    w_specs = [pl.BlockSpec((None,) + w.shape[1:], lambda r: (r, 0, 0)) for w in weights]
    stats_spec = pl.BlockSpec((None,) + stats.shape[1:], lambda r: (r, 0, 0))
    og_spec = pl.BlockSpec((tbb, nq, GROUP_W), lambda r: (r, 0, 0))
    operands += [*caches, *weights, stats, *half_shifted]
    first_aliased = len(operands) - len(half_shifted)
    return pl.pallas_call(
        functools.partial(_attn_kernel, blocks_per_seq=blocks_per_seq),
        grid=(n_req,),
        in_specs=[_resident(qkvt.shape)] + a_specs + v_specs + w_specs + [stats_spec]
        + [pl.BlockSpec(memory_space=pl.ANY)] * len(half_shifted),
        out_specs=[og_spec] * (2 * N_GROUPS) + v_specs + [pl.BlockSpec((D_ATTN, n_req), lambda r: (0, 0))],
        out_shape=[jax.ShapeDtypeStruct((n_blocks, nq, GROUP_W), BF16)] * N_GROUPS
        + [jax.ShapeDtypeStruct((n_blocks, nq, GROUP_W), F32)] * N_GROUPS
        + [jax.ShapeDtypeStruct(c.shape, F32) for c in caches]
        + [jax.ShapeDtypeStruct((D_ATTN, n_req), F32)],
        input_output_aliases={first_aliased + g: 2 * N_GROUPS + g for g in range(len(half_shifted))},
        compiler_params=pltpu.CompilerParams(dimension_semantics=("arbitrary",),
                                             vmem_limit_bytes=VMEM_LIMIT_BYTES),
        name="k_attn",
    )(*operands)


def _merge_groups(o_and_lse):
    m = o_and_lse[0][1]
    for _, lse in o_and_lse[1:]:
        m = jnp.maximum(m, lse)
    es = [jnp.exp(lse - m) for _, lse in o_and_lse]
    inv = 1.0 / (es[0] + es[1] + es[2])
    return jnp.concatenate([(o * (e * inv)).astype(BF16) for (o, _), e in zip(o_and_lse, es)], axis=1)


def _mix_tail(x1, gg, u, mixed, attn, mod_ref, ng_ref, wba_ref, wbb_ref, wo_ref,
              wu_ref, wd_ref, fg_ref, hb_scr, acc_scr):
    gm = (u * mixed).astype(BF16)
    merged = (_sigmoid(gg[:, :D_MODEL]) * jnp.dot(attn, wba_ref[...], preferred_element_type=F32)
              + _sigmoid(gg[:, D_MODEL:]) * jnp.dot(gm, wbb_ref[...], preferred_element_type=F32))
    y = jnp.dot(merged.astype(BF16), wo_ref[...], preferred_element_type=F32)
    x2 = x1 + _mod(mod_ref, 5) * y
    hb_scr[...] = _pre(x2, ng_ref[2:3, :], mod_ref, 2).astype(BF16)
    f = _swiglu(hb_scr, wu_ref, wd_ref, acc_scr)
    x3 = x2 + (0.5 * _mod(mod_ref, 8)) * f
    return _rms(x3) * fg_ref[...]


def _layer_norm_rows(v, g_ref, b_ref):
    mu = jnp.mean(v, axis=-1, keepdims=True)
    var = jnp.mean(jnp.square(v - mu), axis=-1, keepdims=True)
    return ((v - mu) * lax.rsqrt(var + EPS)) * g_ref[...] + b_ref[...]


def _mix_ffn_prompt_kernel(x1_ref, ob0_ref, ob1_ref, ob2_ref, ls0_ref, ls1_ref, ls2_ref, mod_ref, ng_ref, wr_ref,
                           lng_ref, lnb_ref,
                           ws_ref, bst_ref, wba_ref, wbb_ref, wo_ref, wu_ref, wd_ref, fg_ref,
                           y_ref, vrows_ref, hb_scr, acc_scr, un_scr, mixed_scr):
    tm = x1_ref.shape[0]
    x1 = x1_ref[...]
    h1 = _pre(x1, ng_ref[1:2, :], mod_ref, 1).astype(BF16)
    uv = jnp.dot(h1, wr_ref[:, :2 * D_GMLP], preferred_element_type=F32)
    gg = jnp.dot(h1, wr_ref[:, 2 * D_GMLP:], preferred_element_type=F32)
    u = uv[:, :D_GMLP]
    vn = _layer_norm_rows(uv[:, D_GMLP:], lng_ref, lnb_ref)

    vrows_ref[...] = vn[tm - CHUNK:, :]

    vnb = vn.astype(BF16)
    tri = (lax.broadcasted_iota(jnp.int32, (CHUNK, CHUNK), 0)
           >= lax.broadcasted_iota(jnp.int32, (CHUNK, CHUNK), 1))
    gw = D_GMLP // GMLP_GROUPS
    for g in range(GMLP_GROUPS):
        ws = jnp.where(tri, ws_ref[g], 0.0).astype(BF16)
        bias = bst_ref[:, g:g + 1]
        for c in range(tm // CHUNK):
            mixed_scr[c * CHUNK:(c + 1) * CHUNK, g * gw:(g + 1) * gw] = (
                jnp.dot(ws, vnb[c * CHUNK:(c + 1) * CHUNK, g * gw:(g + 1) * gw],
                        preferred_element_type=F32) + bias)

    half = GROUP_W // LANES
    for g, (ob_ref, lse_ref) in enumerate(((ob0_ref, ls0_ref), (ob1_ref, ls1_ref), (ob2_ref, ls2_ref))):
        dil = DIL_GROUPS[g][1]
        n = tm // dil
        for r in range(dil):
            for c in range(half):
                lanes = slice(c * LANES, (c + 1) * LANES)
                un_scr[g, c, pl.ds(r, n, stride=dil), :] = ob_ref[r, :, lanes].astype(F32)
                un_scr[g, half + c, pl.ds(r, n, stride=dil), :] = lse_ref[r, :, lanes]
    o_and_lse = []
    for g in range(N_GROUPS):
        cols = [un_scr[g, c] for c in range(2 * half)]
        o_and_lse.append((jnp.concatenate(cols[:half], axis=1), jnp.concatenate(cols[half:], axis=1)))
    attn = _merge_groups(o_and_lse)

    y_ref[...] = _mix_tail(x1, gg, u, mixed_scr[...], attn, mod_ref, ng_ref, wba_ref, wbb_ref, wo_ref,
                           wu_ref, wd_ref, fg_ref, hb_scr, acc_scr)


def _mix_ffn_prompt(x1, obs, lses, mod, norm_g, wr, lng, lnb, ws, bst, wba, wbb, wo, wu, wd, fg):
    bsz, seq_len, _ = x1.shape
    tm = ROW_TILE
    weights = (norm_g, wr, lng, lnb, ws, bst, wba, wbb, wo, wu, wd, fg)
    row_spec = pl.BlockSpec((None, tm, D_MODEL), lambda b, i: (b, i, 0))
    og_specs = [pl.BlockSpec((None, dil, tm // dil, GROUP_W), lambda b, i: (b, 0, i, 0))
                for _, dil in DIL_GROUPS] * 2
    return pl.pallas_call(
        _mix_ffn_prompt_kernel,
        grid=(bsz, seq_len // tm),
        in_specs=[row_spec] + og_specs + [pl.BlockSpec((None,) + mod.shape[1:], lambda b, i: (b, 0, 0))]
        + [_resident(w.shape) for w in weights],
        out_specs=[row_spec, pl.BlockSpec((None, CHUNK, D_GMLP), lambda b, i: (b, 0, 0))],
        out_shape=[jax.ShapeDtypeStruct(x1.shape, F32), jax.ShapeDtypeStruct((bsz, CHUNK, D_GMLP), F32)],
        scratch_shapes=[pltpu.VMEM((tm, D_MODEL), BF16), pltpu.VMEM((tm, D_MODEL), F32),
                        pltpu.VMEM((N_GROUPS, 2 * GROUP_W // LANES, tm, LANES), F32),
                        pltpu.VMEM((tm, D_GMLP), F32)],
        compiler_params=pltpu.CompilerParams(dimension_semantics=("arbitrary", "arbitrary"),
                                             vmem_limit_bytes=VMEM_LIMIT_BYTES),
        name="k_mix_ffn",
    )(x1, *obs, *lses, mod, *weights)


def _mix_ffn_sample_kernel(x1_ref, attnt_ref, mod_ref, ng_ref, wr_ref, lng_ref, lnb_ref,
                           ws0_ref, bs0_ref, wba_ref, wbb_ref, wo_ref, wu_ref, wd_ref, fg_ref,
                           y_ref, vrows_ref, hb_scr, acc_scr):
    x1 = x1_ref[...]
    h1 = _pre(x1, ng_ref[1:2, :], mod_ref, 1).astype(BF16)
    uv = jnp.dot(h1, wr_ref[:, :2 * D_GMLP], preferred_element_type=F32)
    gg = jnp.dot(h1, wr_ref[:, 2 * D_GMLP:], preferred_element_type=F32)
    u = uv[:, :D_GMLP]
    vn = _layer_norm_rows(uv[:, D_GMLP:], lng_ref, lnb_ref)
    vrows_ref[...] = vn
    mixed = ws0_ref[...] * vn + bs0_ref[...]
    attn = attnt_ref[...].T.astype(BF16)
    y_ref[...] = _mix_tail(x1, gg, u, mixed, attn, mod_ref, ng_ref, wba_ref, wbb_ref, wo_ref,
                           wu_ref, wd_ref, fg_ref, hb_scr, acc_scr)


def _mix_ffn_sample(x1, attnt, mod, norm_g, wr, lng, lnb, ws0, bs0, wba, wbb, wo, wu, wd, fg):
    n = x1.shape[0]
    operands = (x1, attnt, mod, norm_g, wr, lng, lnb, ws0, bs0, wba, wbb, wo, wu, wd, fg)
    return pl.pallas_call(
        _mix_ffn_sample_kernel,
        grid=(1,),
        in_specs=[_resident(w.shape) for w in operands],
        out_specs=[_whole(x1.shape), _whole((n, D_GMLP))],
        out_shape=[jax.ShapeDtypeStruct(x1.shape, F32), jax.ShapeDtypeStruct((n, D_GMLP), F32)],
        scratch_shapes=[pltpu.VMEM((n, D_MODEL), BF16), pltpu.VMEM((n, D_MODEL), F32)],
        compiler_params=pltpu.CompilerParams(dimension_semantics=("arbitrary",),
                                             vmem_limit_bytes=VMEM_LIMIT_BYTES),
        name="k_mix_ffn_sample",
    )(*operands)


def _to_buffer_layout(cache):
    n, rows = cache.shape[0], cache.shape[1]
    return cache.transpose(0, 2, 3, 4, 1).reshape(n, 2 * HEADS_PER_GROUP, HEAD_DIM, rows)


def _from_buffer_layout(buf):
    n, rows = buf.shape[0], buf.shape[-1]
    return buf.reshape(n, 2, HEADS_PER_GROUP, HEAD_DIM, rows).transpose(0, 4, 1, 2, 3)[None]


def kernel(x_prompt, x_sample, c_prompt, c_sample, cache_kv_g0, cache_kv_g1, cache_kv_g2, ada_w, ada_b, norm_g,
           ffn1_up, ffn1_down, w_in, w_branch_a, w_branch_b, w_out, v_ln_g, v_ln_b, w_spatial, b_spatial,
           ffn2_up, ffn2_down, final_g):
    assert ada_w.shape[0] == 1, "one layer"
    bsz, seq_len, _ = x_prompt.shape
    n_req = x_sample.shape[0]

    wu1, wd1 = ffn1_up[0].astype(BF16), ffn1_down[0].astype(BF16)
    wu2, wd2 = ffn2_up[0].astype(BF16), ffn2_down[0].astype(BF16)
    wqkv = w_in[0][:, :3 * D_ATTN].astype(BF16)
    wr = w_in[0][:, 3 * D_ATTN:].astype(BF16)
    wkvt = wqkv[:, D_ATTN:].T
    wba, wbb, wo = w_branch_a[0].astype(BF16), w_branch_b[0].astype(BF16), w_out[0].astype(BF16)
    ng = norm_g[0]
    lng, lnb = v_ln_g, v_ln_b
    ws = w_spatial[0]
    bst = b_spatial[0].T
    gw = D_GMLP // GMLP_GROUPS
    ws0 = jnp.repeat(ws[:, 0, 0], gw)[None, :]
    bs0 = jnp.repeat(b_spatial[0][:, 0], gw)[None, :]
    fg = final_g[None, :]

    mods = _ada_table(jnp.concatenate([c_prompt, c_sample], axis=0), ada_w[0], ada_b)
    mod_p = mods[:, :bsz].transpose(1, 0, 2)
    mod_s = mods[:, bsz:]

    x1s, qkvt = _ffn_qkv_sample(x_sample.reshape(n_req, D_MODEL), mod_s, ng, wu1, wd1, wqkv)
    caches = [_to_buffer_layout(c[0]) for c in (cache_kv_g0, cache_kv_g1, cache_kv_g2)]
    (x1, a0, a1, a2, kv0, kv1, kv2, nk0, nk1, nk2, pn0, pn1, pn2, stats) = _ffn_qkv_prompt(
        x_prompt, mod_p, ng, wu1, wd1, wqkv, wkvt, qkvt, caches)

    nq = KEYS_PER_QUERY_BLOCK
    a_blocks = [a.reshape(-1, nq, 3 * GROUP_W) for a in (a0, a1, a2)]
    blocks_per_seq = tuple(seq_len // dil // nq for _, dil in DIL_GROUPS)
    ob0, ob1, ob2, ls0, ls1, ls2, s0, s1, s2, attnt = _attention(
        qkvt, a_blocks, caches, (nk0, nk1, nk2), (pn0, pn1, pn2), stats, blocks_per_seq)
    per_residue = lambda og, dil: og.reshape(bsz, dil, seq_len // dil, GROUP_W)
    obs = [per_residue(og, dil) for og, (_, dil) in zip((ob0, ob1, ob2), DIL_GROUPS)]
    lses = [per_residue(og, dil) for og, (_, dil) in zip((ls0, ls1, ls2), DIL_GROUPS)]

    y_prompt, vrows_p = _mix_ffn_prompt(x1, obs, lses, mod_p, ng, wr, lng, lnb, ws, bst, wba, wbb, wo,
                                        wu2, wd2, fg)
    y_sample, vrows_s = _mix_ffn_sample(x1s, attnt, mod_s, ng, wr, lng, lnb, ws0, bs0, wba, wbb, wo,
                                        wu2, wd2, fg)

    return (y_prompt, y_sample.reshape(n_req, 1, D_MODEL),
            _from_buffer_layout(kv0), _from_buffer_layout(kv1), _from_buffer_layout(kv2), vrows_p[None],
            _from_buffer_layout(s0), _from_buffer_layout(s1), _from_buffer_layout(s2),
            vrows_s.reshape(1, n_req, 1, D_GMLP))
```

```python
import functools

import jax
import jax.numpy as jnp
from jax import lax
from jax.experimental import pallas as pl
from jax.experimental.pallas import tpu as pltpu

F32 = jnp.float32
BF16 = jnp.bfloat16

D_MODEL = 1024
HEAD_DIM = 64
HEADS_PER_GROUP = 4
HEADS_PER_PRODUCT = 2
GROUP_W = HEADS_PER_GROUP * HEAD_DIM
DIL_GROUPS = ((128, 1), (512, 4), (2048, 16))
N_GROUPS = len(DIL_GROUPS)
D_ATTN = N_GROUPS * GROUP_W
D_GMLP = 512
GMLP_GROUPS = 4
CHUNK = 128
D_FF = 2816
FF_CHUNK = 256
N_FF_CHUNKS = D_FF // FF_CHUNK
EPS = 1e-6
QK_SCALE = HEAD_DIM ** -0.5
KEYS_PER_QUERY_BLOCK = 128
MASKED = -1e30
LANES = 128

VMEM_LIMIT_BYTES = 60 * 1024 * 1024
ROW_TILE = 512


def _resident(shape):
    zeros = (0,) * len(shape)
    return pl.BlockSpec(shape, lambda *_: zeros, pipeline_mode=pl.Buffered(1))


def _whole(shape):
    zeros = (0,) * len(shape)
    return pl.BlockSpec(shape, lambda *_: zeros)


def _sigmoid(x):
    return 0.5 * jnp.tanh(0.5 * x) + 0.5


def _rms(x):
    return x * lax.rsqrt(jnp.mean(x * x, axis=-1, keepdims=True) + EPS)


def _mod(mod_ref, i):
    if len(mod_ref.shape) == 2:
        return mod_ref[i:i + 1, :]
    return mod_ref[i]


def _pre(x, g, mod_ref, sub):
    scale = _mod(mod_ref, 3 * sub + 1)
    if scale.shape[0] == 1:
        return _rms(x) * (g * (1.0 + scale)) + _mod(mod_ref, 3 * sub)
    return (_rms(x) * g) * (1.0 + scale) + _mod(mod_ref, 3 * sub)


def _swiglu_chunk(c, h_ref, wu_ref, wd_ref, acc_ref):
    lo, hi = c * FF_CHUNK, (c + 1) * FF_CHUNK
    h = h_ref[...]
    a = jnp.dot(h, wu_ref[:, lo:hi], preferred_element_type=F32)
    b = jnp.dot(h, wu_ref[:, D_FF + lo:D_FF + hi], preferred_element_type=F32)
    act = ((a * _sigmoid(a)) * b).astype(BF16)
    part = jnp.dot(act, wd_ref[lo:hi, :], preferred_element_type=F32)
    if c == 0:
        acc_ref[...] = part
    else:
        acc_ref[...] += part


def _swiglu(h_ref, wu_ref, wd_ref, acc_ref, side_tasks=()):
    dealt = 0
    for c in range(N_FF_CHUNKS):
        _swiglu_chunk(c, h_ref, wu_ref, wd_ref, acc_ref)
        while dealt * N_FF_CHUNKS < len(side_tasks) * (c + 1):
            side_tasks[dealt]()
            dealt += 1
    return acc_ref[...]


def _ada_kernel(c_ref, w_ref, b_ref, o_ref):
    c = c_ref[...]
    h = (c * _sigmoid(c)).astype(BF16)
    o_ref[...] = jnp.dot(h, w_ref[...].astype(BF16), preferred_element_type=F32) + b_ref[...]


def _ada_table(c_all, ada_w, ada_b):
    n = c_all.shape[0]
    n_ada = ada_w.shape[1] // D_MODEL
    return pl.pallas_call(
        _ada_kernel,
        grid=(n_ada,),
        in_specs=[
            pl.BlockSpec((n, D_MODEL), lambda i: (0, 0)),
            pl.BlockSpec((D_MODEL, D_MODEL), lambda i: (0, i)),
            pl.BlockSpec((1, D_MODEL), lambda i: (0, i)),
        ],
        out_specs=pl.BlockSpec((None, n, D_MODEL), lambda i: (i, 0, 0)),
        out_shape=jax.ShapeDtypeStruct((n_ada, n, D_MODEL), F32),
        compiler_params=pltpu.CompilerParams(dimension_semantics=("arbitrary",)),
        name="k_ada",
    )(c_all, ada_w, ada_b)


def _ffn1_front(x_ref, mod_ref, ng_ref, wu_ref, wd_ref, hb_scr, acc_scr, side_tasks=()):
    x = x_ref[...]
    hb_scr[...] = _pre(x, ng_ref[0:1, :], mod_ref, 0).astype(BF16)
    f = _swiglu(hb_scr, wu_ref, wd_ref, acc_scr, side_tasks)
    x1 = x + (0.5 * _mod(mod_ref, 2)) * f
    return x1, _pre(x1, ng_ref[1:2, :], mod_ref, 1)


def _ffn_qkv_prompt_kernel(x_ref, mod_ref, ng_ref, wu_ref, wd_ref, wqkv_ref, wkvt_ref,
                           qkvt_ref, ck0_ref, ck1_ref, ck2_ref,
                           x1_ref, a0_ref, a1_ref, a2_ref, kv0_ref, kv1_ref, kv2_ref,
                           nk0_ref, nk1_ref, nk2_ref, pn0_ref, pn1_ref, pn2_ref, stats_ref,
                           hb_scr, acc_scr, h1_scr, hp_scr, *, seq_len):
    tm = x_ref.shape[0]
    i = pl.program_id(1)

    request = pl.program_id(0) * pl.num_programs(1) + i
    onehot = (lax.broadcasted_iota(jnp.int32, (1, qkvt_ref.shape[1]), 1) == request).astype(F32)
    cols = jnp.sum(qkvt_ref[0:2 * D_ATTN, :] * onehot, axis=1, keepdims=True)

    def scores_task(g, h, c_ref, n_ref, pn_ref):
        p, p_new, l, lse = _sample_scores(g, h, cols, c_ref, n_ref)
        inv_l = 1.0 / l
        pn_ref[h:h + 1, :] = p * inv_l
        row = g * HEADS_PER_GROUP + h
        stats_ref[row:row + 1, :] = jnp.broadcast_to(p_new * inv_l, (1, LANES))
        n_heads = N_GROUPS * HEADS_PER_GROUP
        stats_ref[n_heads + row:n_heads + row + 1, :] = jnp.broadcast_to(lse, (1, LANES))

    side_tasks = [functools.partial(scores_task, g, h, c_ref, n_ref, pn_ref)
                  for g, (c_ref, n_ref, pn_ref) in enumerate(((ck0_ref, nk0_ref, pn0_ref), (ck1_ref, nk1_ref, pn1_ref),
                                                              (ck2_ref, nk2_ref, pn2_ref)))
                  for h in range(HEADS_PER_GROUP)]
    x1, h1 = _ffn1_front(x_ref, mod_ref, ng_ref, wu_ref, wd_ref, hb_scr, acc_scr, side_tasks)
    x1_ref[...] = x1
    for c in range(D_MODEL // LANES):
        h1_scr[c] = h1[:, c * LANES:(c + 1) * LANES]
    hb_scr[...] = h1.astype(BF16)

    for g, a_ref in enumerate((a0_ref, a1_ref, a2_ref)):
        dil = DIL_GROUPS[g][1]
        n = tm // dil
        if dil == 1:
            hp = hb_scr[...]
        else:
            for r in range(dil):
                rows = [h1_scr[c, pl.ds(r, n, stride=dil), :] for c in range(D_MODEL // LANES)]
                hp_scr[r * n:(r + 1) * n, :] = jnp.concatenate(rows, axis=1).astype(BF16)
            hp = hp_scr[...]
        for part in range(3):
            lo = part * D_ATTN + g * GROUP_W
            res = jnp.dot(hp, wqkv_ref[:, lo:lo + GROUP_W], preferred_element_type=F32)
            if part == 0:
                res = res * QK_SCALE
            a_ref[:, :, part * GROUP_W:(part + 1) * GROUP_W] = res.astype(BF16).reshape(dil, n, GROUP_W)

    for g, kv_ref in enumerate((kv0_ref, kv1_ref, kv2_ref)):
        window = DIL_GROUPS[g][0]
        first_tile = (seq_len - window) // tm

        @pl.when(i >= first_tile)
        def _(g=g, kv_ref=kv_ref, window=window):
            for part in range(2):
                lo = part * D_ATTN + g * GROUP_W
                t = lax.dot_general(wkvt_ref[lo:lo + GROUP_W, :], hb_scr[...], (((1,), (1,)), ((), ())),
                                    preferred_element_type=F32)
                if window < tm:
                    t = t[:, tm - window:]
                kv_ref[part * HEADS_PER_GROUP:(part + 1) * HEADS_PER_GROUP] = t.reshape(
                    HEADS_PER_GROUP, HEAD_DIM, t.shape[-1])


def _ffn_qkv_prompt(x, mod, norm_g, wu, wd, wqkv, wkvt, qkvt, caches):
    bsz, seq_len, _ = x.shape
    tm = ROW_TILE
    nt = seq_len // tm
    n_req = qkvt.shape[1]
    assert bsz * nt == n_req, "one sample request rides on each row-tile grid step"
    weights = (norm_g, wu, wd, wqkv, wkvt, qkvt)
    n_heads = N_GROUPS * HEADS_PER_GROUP
    k_half = lambda c: pl.BlockSpec((None, HEADS_PER_GROUP) + c.shape[2:], lambda b, i: (b * nt + i, 0, 0, 0))
    side_in_specs = [k_half(c) for c in caches]
    side_out_specs = ([k_half(c) for c in caches]
                      + [pl.BlockSpec((None, HEADS_PER_GROUP, c.shape[-1]), lambda b, i: (b * nt + i, 0, 0))
                         for c in caches]
                      + [pl.BlockSpec((None, 2 * n_heads, LANES), lambda b, i: (b * nt + i, 0, 0))])
    side_out_shapes = ([jax.ShapeDtypeStruct(c.shape, F32) for c in caches]
                       + [jax.ShapeDtypeStruct((n_req, HEADS_PER_GROUP, c.shape[-1]), F32) for c in caches]
                       + [jax.ShapeDtypeStruct((n_req, 2 * n_heads, LANES), F32)])
    a_shapes, a_specs, kv_shapes, kv_specs = [], [], [], []
    for window, dil in DIL_GROUPS:
        a_shapes.append(jax.ShapeDtypeStruct((bsz, dil, seq_len // dil, 3 * GROUP_W), BF16))
        a_specs.append(pl.BlockSpec((None, dil, tm // dil, 3 * GROUP_W), lambda b, i: (b, 0, i, 0)))
        first_tile = (seq_len - window) // tm
        kv_shapes.append(jax.ShapeDtypeStruct((bsz, 2 * HEADS_PER_GROUP, HEAD_DIM, window), F32))
        kv_specs.append(pl.BlockSpec(
            (None, 2 * HEADS_PER_GROUP, HEAD_DIM, min(window, tm)),
            lambda b, i, first_tile=first_tile: (b, 0, 0, jnp.maximum(i - first_tile, 0))))
    row_spec = pl.BlockSpec((None, tm, D_MODEL), lambda b, i: (b, i, 0))
    return pl.pallas_call(
        functools.partial(_ffn_qkv_prompt_kernel, seq_len=seq_len),
        grid=(bsz, seq_len // tm),
        in_specs=[row_spec, pl.BlockSpec((None,) + mod.shape[1:], lambda b, i: (b, 0, 0))]
        + [_resident(w.shape) for w in weights] + side_in_specs,
        out_specs=[row_spec] + a_specs + kv_specs + side_out_specs,
        out_shape=[jax.ShapeDtypeStruct(x.shape, F32)] + a_shapes + kv_shapes + side_out_shapes,
        scratch_shapes=[pltpu.VMEM((tm, D_MODEL), BF16), pltpu.VMEM((tm, D_MODEL), F32),
                        pltpu.VMEM((D_MODEL // LANES, tm, LANES), F32), pltpu.VMEM((tm, D_MODEL), BF16)],
        compiler_params=pltpu.CompilerParams(dimension_semantics=("arbitrary", "arbitrary"),
                                             vmem_limit_bytes=VMEM_LIMIT_BYTES),
        name="k_ffn_qkv",
    )(x, mod, *weights, *caches)


def _ffn_qkv_sample_kernel(x_ref, mod_ref, ng_ref, wu_ref, wd_ref, wqkv_ref,
                           x1_ref, qkvt_ref, hb_scr, acc_scr):
    x1, h1 = _ffn1_front(x_ref, mod_ref, ng_ref, wu_ref, wd_ref, hb_scr, acc_scr)
    x1_ref[...] = x1
    qkv = jnp.dot(h1.astype(BF16), wqkv_ref[...], preferred_element_type=F32)
    col = lax.broadcasted_iota(jnp.int32, (1, 3 * D_ATTN), 1)
    qkv = qkv * jnp.where(col < D_ATTN, QK_SCALE, 1.0)
    qkvt_ref[...] = qkv.T


def _ffn_qkv_sample(x, mod, norm_g, wu, wd, wqkv):
    n = x.shape[0]
    weights = (norm_g, wu, wd, wqkv)
    return pl.pallas_call(
        _ffn_qkv_sample_kernel,
        grid=(1,),
        in_specs=[_resident(x.shape), _resident(mod.shape)] + [_resident(w.shape) for w in weights],
        out_specs=[_whole(x.shape), _whole((3 * D_ATTN, n))],
        out_shape=[jax.ShapeDtypeStruct(x.shape, F32), jax.ShapeDtypeStruct((3 * D_ATTN, n), F32)],
        scratch_shapes=[pltpu.VMEM((n, D_MODEL), BF16), pltpu.VMEM((n, D_MODEL), F32)],
        compiler_params=pltpu.CompilerParams(dimension_semantics=("arbitrary",),
                                             vmem_limit_bytes=VMEM_LIMIT_BYTES),
        name="k_ffn_qkv_sample",
    )(x, mod, *weights)


def _band_scores(q, k2, lane_head):
    pairs = []
    for first in range(0, HEADS_PER_GROUP, HEADS_PER_PRODUCT):
        qs = jnp.concatenate([jnp.where(lane_head == h, q, jnp.zeros_like(q))
                              for h in range(first, first + HEADS_PER_PRODUCT)], axis=0)
        pairs.append(lax.dot_general(qs, k2, (((1,), (1,)), ((), ())), preferred_element_type=F32))
    return pairs


def _band_softmax_values(score_pairs, v2, first_key, band_key, lane_head, ones):
    nq = KEYS_PER_QUERY_BLOCK
    o = lse_b = None
    for pair, s in enumerate(score_pairs):
        s = jnp.where(band_key >= first_key, s, MASKED)
        m = jnp.max(s, axis=1, keepdims=True)
        p = jnp.exp(s - m).astype(BF16)
        r = jnp.dot(p, v2, preferred_element_type=F32)
        l = jnp.dot(p, ones, preferred_element_type=F32)
        inv_l = 1.0 / l
        lse = m + jnp.log(l)
        for j in range(HEADS_PER_PRODUCT):
            h = pair * HEADS_PER_PRODUCT + j
            rows = slice(j * nq, (j + 1) * nq)
            o_h = r[rows] * jnp.concatenate([inv_l[rows]] * (GROUP_W // LANES), axis=1)
            lse_h = jnp.concatenate([lse[rows]] * (GROUP_W // LANES), axis=1)
            o = o_h if h == 0 else jnp.where(lane_head == h, o_h, o)
            lse_b = lse_h if h == 0 else jnp.where(lane_head == h, lse_h, lse_b)
    return o, lse_b


def _shifted(buf, newest, new_col):
    return jnp.where(newest, new_col, pltpu.roll(buf, buf.shape[-1] - 1, axis=1))


def _sample_scores(g, h, cols, c_ref, n_ref):
    window, dil = DIL_GROUPS[g]
    rows = c_ref.shape[-1]
    col = lax.broadcasted_iota(jnp.int32, (1, rows), 1)
    dist = rows - col
    strided = (dist % dil == 0) & (dist <= window)
    base = g * GROUP_W + h * HEAD_DIM
    qc = cols[base:base + HEAD_DIM]
    kc = cols[D_ATTN + base:D_ATTN + base + HEAD_DIM]
    kt = c_ref[h]
    s = jnp.where(strided, jnp.sum(kt * qc, axis=0, keepdims=True), MASKED)
    s_new = jnp.sum(kc * qc, axis=0, keepdims=True)
    m = jnp.maximum(jnp.max(s, axis=1, keepdims=True), s_new)
    p = jnp.exp(s - m)
    p_new = jnp.exp(s_new - m)
    l = jnp.sum(p, axis=1, keepdims=True) + p_new
    n_ref[h] = _shifted(kt, col == rows - 1, kc)
    return p, p_new, l, m + jnp.log(l)


def _sample_values(g, h, v_cols, c_ref, n_ref, pn, pn_new):
    rows = c_ref.shape[-1]
    col = lax.broadcasted_iota(jnp.int32, (1, rows), 1)
    base = g * GROUP_W + h * HEAD_DIM
    vc = v_cols[base:base + HEAD_DIM]
    vt = c_ref[h]
    n_ref[h] = _shifted(vt, col == rows - 1, vc)
    return jnp.sum(vt * pn, axis=1, keepdims=True) + vc * pn_new


def _attn_kernel(qkvt_ref, a0_ref, p0_ref, a1_ref, p1_ref, a2_ref, p2_ref, c0_ref, c1_ref, c2_ref,
                 pn0_ref, pn1_ref, pn2_ref, stats_ref, nk0_ref, nk1_ref, nk2_ref,
                 ob0_ref, ob1_ref, ob2_ref, ls0_ref, ls1_ref, ls2_ref, n0_ref, n1_ref, n2_ref, attnt_ref,
                 *, blocks_per_seq):
    del nk0_ref, nk1_ref, nk2_ref
    step = pl.program_id(0)
    nq = KEYS_PER_QUERY_BLOCK

    @pl.when(step == 0)
    def _():
        attnt_ref[...] = jnp.zeros_like(attnt_ref)

    lane_head = lax.broadcasted_iota(jnp.int32, (nq, GROUP_W), 1) // HEAD_DIM
    qi = lax.broadcasted_iota(jnp.int32, (HEADS_PER_PRODUCT * nq, 2 * nq), 0) % nq
    kj = lax.broadcasted_iota(jnp.int32, (HEADS_PER_PRODUCT * nq, 2 * nq), 1)
    band_key = jnp.where((kj >= qi) & (kj <= qi + nq), kj, -1)
    ones = jnp.ones((2 * nq, LANES), BF16)

    scores = {}

    def scores_of(g, t, a_ref, p_ref, og_ref):
        before = p_ref[0] if t == 0 else a_ref[t - 1]
        own = a_ref[t]
        k2 = jnp.concatenate([before[:, GROUP_W:2 * GROUP_W], own[:, GROUP_W:2 * GROUP_W]], axis=0)
        scores[g, t] = _band_scores(own[:, 0:GROUP_W], k2, lane_head)

    def band_task(g, t, a_ref, p_ref, og_ref):
        tbb = a_ref.shape[0]
        before = p_ref[0] if t == 0 else a_ref[t - 1]
        v2 = jnp.concatenate([before[:, 2 * GROUP_W:], a_ref[t][:, 2 * GROUP_W:]], axis=0)
        first_key = jnp.where(((step * tbb + t) % blocks_per_seq[g]) == 0, nq, 0)
        o, lse_b = _band_softmax_values(scores.pop((g, t)), v2, first_key, band_key, lane_head, ones)
        ob_ref, ls_ref = og_ref
        ob_ref[t] = o.astype(BF16)
        ls_ref[t] = lse_b

    blocks = [(g, t, a_ref, p_ref, og_ref)
              for g, (a_ref, p_ref, og_ref) in enumerate(((a0_ref, p0_ref, (ob0_ref, ls0_ref)),
                                                          (a1_ref, p1_ref, (ob1_ref, ls1_ref)),
                                                          (a2_ref, p2_ref, (ob2_ref, ls2_ref))))
              for t in range(a_ref.shape[0])]
    score_ahead = 2
    for blk in blocks[:score_ahead]:
        scores_of(*blk)

    def make_band_task(i):
        def task():
            band_task(*blocks[i])
            if i + score_ahead < len(blocks):
                scores_of(*blocks[i + score_ahead])
        return task

    band_tasks = [make_band_task(i) for i in range(len(blocks))]

    n_req = qkvt_ref.shape[1]
    onehot = (lax.broadcasted_iota(jnp.int32, (1, n_req), 1) == step).astype(F32)
    v_cols = jnp.sum(qkvt_ref[2 * D_ATTN:, :] * onehot, axis=1, keepdims=True)
    n_heads = N_GROUPS * HEADS_PER_GROUP
    heads = {}

    def values_task(g, h, c_ref, n_ref, pn_ref):
        row = g * HEADS_PER_GROUP + h
        pn_new = stats_ref[row:row + 1, 0:1]
        lse = stats_ref[n_heads + row:n_heads + row + 1, 0:1]
        heads[g, h] = (_sample_values(g, h, v_cols, c_ref, n_ref, pn_ref[h:h + 1, :], pn_new), lse)

    sample_tasks = []
    for g, (c_ref, n_ref, pn_ref) in enumerate(((c0_ref, n0_ref, pn0_ref), (c1_ref, n1_ref, pn1_ref),
                                                (c2_ref, n2_ref, pn2_ref))):
        for h in range(HEADS_PER_GROUP):
            sample_tasks.append((c_ref.shape[-1], functools.partial(values_task, g, h, c_ref, n_ref, pn_ref)))

    total = sum(cost for cost, _ in sample_tasks)
    dealt = nxt = 0
    for i, task in enumerate(band_tasks):
        task()
        while nxt < len(sample_tasks) and dealt * len(band_tasks) < total * (i + 1):
            cost, sample_task = sample_tasks[nxt]
            sample_task()
            dealt += cost
            nxt += 1
    for _, sample_task in sample_tasks[nxt:]:
        sample_task()

    for h in range(HEADS_PER_GROUP):
        lses = [heads[g, h][1] for g in range(N_GROUPS)]
        m = jnp.maximum(jnp.maximum(lses[0], lses[1]), lses[2])
        es = [jnp.exp(lse - m) for lse in lses]
        inv = 1.0 / (es[0] + es[1] + es[2])
        for g in range(N_GROUPS):
            base = g * GROUP_W + h * HEAD_DIM
            attnt_ref[base:base + HEAD_DIM, :] += (heads[g, h][0] * (es[g] * inv)) * onehot


def _attention(qkvt, a_blocks, caches, half_shifted, weights, stats, blocks_per_seq):
    n_req = qkvt.shape[1]
    n_blocks = a_blocks[0].shape[0]
    assert n_blocks % n_req == 0, "query blocks are dealt evenly over the per-request grid steps"
    tbb = n_blocks // n_req
    nq = KEYS_PER_QUERY_BLOCK
    a_specs, operands = [], [qkvt]
    for a in a_blocks:
        a_specs += [pl.BlockSpec((tbb, nq, 3 * GROUP_W), lambda r: (r, 0, 0)),
                    pl.BlockSpec((1, nq, 3 * GROUP_W), lambda r: (jnp.maximum(r * tbb - 1, 0), 0, 0))]
        operands += [a, a]
    v_specs = [pl.BlockSpec((None, HEADS_PER_GROUP) + c.shape[2:], lambda r: (r, 1, 0, 0)) for c in caches]
    w_specs = [pl.BlockSpec((None,) + w.shape[1:], lambda r: (r, 0, 0)) for w in weights]
    stats_spec = pl.BlockSpec((None,) + stats.shape[1:], lambda r: (r, 0, 0))
    og_spec = pl.BlockSpec((tbb, nq, GROUP_W), lambda r: (r, 0, 0))
    operands += [*caches, *weights, stats, *half_shifted]
    first_aliased = len(operands) - len(half_shifted)
    return pl.pallas_call(
        functools.partial(_attn_kernel, blocks_per_seq=blocks_per_seq),
        grid=(n_req,),
        in_specs=[_resident(qkvt.shape)] + a_specs + v_specs + w_specs + [stats_spec]
        + [pl.BlockSpec(memory_space=pl.ANY)] * len(half_shifted),
        out_specs=[og_spec] * (2 * N_GROUPS) + v_specs + [pl.BlockSpec((D_ATTN, n_req), lambda r: (0, 0))],
        out_shape=[jax.ShapeDtypeStruct((n_blocks, nq, GROUP_W), BF16)] * N_GROUPS
        + [jax.ShapeDtypeStruct((n_blocks, nq, GROUP_W), F32)] * N_GROUPS
        + [jax.ShapeDtypeStruct(c.shape, F32) for c in caches]
        + [jax.ShapeDtypeStruct((D_ATTN, n_req), F32)],
        input_output_aliases={first_aliased + g: 2 * N_GROUPS + g for g in range(len(half_shifted))},
        compiler_params=pltpu.CompilerParams(dimension_semantics=("arbitrary",),
                                             vmem_limit_bytes=VMEM_LIMIT_BYTES),
        name="k_attn",
    )(*operands)


def _merge_groups(o_and_lse):
    m = o_and_lse[0][1]
    for _, lse in o_and_lse[1:]:
        m = jnp.maximum(m, lse)
    es = [jnp.exp(lse - m) for _, lse in o_and_lse]
    inv = 1.0 / (es[0] + es[1] + es[2])
    return jnp.concatenate([(o * (e * inv)).astype(BF16) for (o, _), e in zip(o_and_lse, es)], axis=1)


def _mix_tail(x1, gg, u, mixed, attn, mod_ref, ng_ref, wba_ref, wbb_ref, wo_ref,
              wu_ref, wd_ref, fg_ref, hb_scr, acc_scr):
    gm = (u * mixed).astype(BF16)
    merged = (_sigmoid(gg[:, :D_MODEL]) * jnp.dot(attn, wba_ref[...], preferred_element_type=F32)
              + _sigmoid(gg[:, D_MODEL:]) * jnp.dot(gm, wbb_ref[...], preferred_element_type=F32))
    y = jnp.dot(merged.astype(BF16), wo_ref[...], preferred_element_type=F32)
    x2 = x1 + _mod(mod_ref, 5) * y
    hb_scr[...] = _pre(x2, ng_ref[2:3, :], mod_ref, 2).astype(BF16)
    f = _swiglu(hb_scr, wu_ref, wd_ref, acc_scr)
    x3 = x2 + (0.5 * _mod(mod_ref, 8)) * f
    return _rms(x3) * fg_ref[...]


def _layer_norm_rows(v, g_ref, b_ref):
    mu = jnp.mean(v, axis=-1, keepdims=True)
    var = jnp.mean(jnp.square(v - mu), axis=-1, keepdims=True)
    return ((v - mu) * lax.rsqrt(var + EPS)) * g_ref[...] + b_ref[...]


def _mix_ffn_prompt_kernel(x1_ref, ob0_ref, ob1_ref, ob2_ref, ls0_ref, ls1_ref, ls2_ref, mod_ref, ng_ref, wr_ref,
                           lng_ref, lnb_ref,
                           ws_ref, bst_ref, wba_ref, wbb_ref, wo_ref, wu_ref, wd_ref, fg_ref,
                           y_ref, vrows_ref, hb_scr, acc_scr, un_scr, mixed_scr):
    tm = x1_ref.shape[0]
    x1 = x1_ref[...]
    h1 = _pre(x1, ng_ref[1:2, :], mod_ref, 1).astype(BF16)
    uv = jnp.dot(h1, wr_ref[:, :2 * D_GMLP], preferred_element_type=F32)
    gg = jnp.dot(h1, wr_ref[:, 2 * D_GMLP:], preferred_element_type=F32)
    u = uv[:, :D_GMLP]
    vn = _layer_norm_rows(uv[:, D_GMLP:], lng_ref, lnb_ref)

    vrows_ref[...] = vn[tm - CHUNK:, :]

    vnb = vn.astype(BF16)
    tri = (lax.broadcasted_iota(jnp.int32, (CHUNK, CHUNK), 0)
           >= lax.broadcasted_iota(jnp.int32, (CHUNK, CHUNK), 1))
    gw = D_GMLP // GMLP_GROUPS
    for g in range(GMLP_GROUPS):
        ws = jnp.where(tri, ws_ref[g], 0.0).astype(BF16)
        bias = bst_ref[:, g:g + 1]
        for c in range(tm // CHUNK):
            mixed_scr[c * CHUNK:(c + 1) * CHUNK, g * gw:(g + 1) * gw] = (
                jnp.dot(ws, vnb[c * CHUNK:(c + 1) * CHUNK, g * gw:(g + 1) * gw],
                        preferred_element_type=F32) + bias)

    half = GROUP_W // LANES
    for g, (ob_ref, lse_ref) in enumerate(((ob0_ref, ls0_ref), (ob1_ref, ls1_ref), (ob2_ref, ls2_ref))):
        dil = DIL_GROUPS[g][1]
        n = tm // dil
        for r in range(dil):
            for c in range(half):
                lanes = slice(c * LANES, (c + 1) * LANES)
                un_scr[g, c, pl.ds(r, n, stride=dil), :] = ob_ref[r, :, lanes].astype(F32)
                un_scr[g, half + c, pl.ds(r, n, stride=dil), :] = lse_ref[r, :, lanes]
    o_and_lse = []
    for g in range(N_GROUPS):
        cols = [un_scr[g, c] for c in range(2 * half)]
        o_and_lse.append((jnp.concatenate(cols[:half], axis=1), jnp.concatenate(cols[half:], axis=1)))
    attn = _merge_groups(o_and_lse)

    y_ref[...] = _mix_tail(x1, gg, u, mixed_scr[...], attn, mod_ref, ng_ref, wba_ref, wbb_ref, wo_ref,
                           wu_ref, wd_ref, fg_ref, hb_scr, acc_scr)


def _mix_ffn_prompt(x1, obs, lses, mod, norm_g, wr, lng, lnb, ws, bst, wba, wbb, wo, wu, wd, fg):
    bsz, seq_len, _ = x1.shape
    tm = ROW_TILE
    weights = (norm_g, wr, lng, lnb, ws, bst, wba, wbb, wo, wu, wd, fg)
    row_spec = pl.BlockSpec((None, tm, D_MODEL), lambda b, i: (b, i, 0))
    og_specs = [pl.BlockSpec((None, dil, tm // dil, GROUP_W), lambda b, i: (b, 0, i, 0))
                for _, dil in DIL_GROUPS] * 2
    return pl.pallas_call(
        _mix_ffn_prompt_kernel,
        grid=(bsz, seq_len // tm),
        in_specs=[row_spec] + og_specs + [pl.BlockSpec((None,) + mod.shape[1:], lambda b, i: (b, 0, 0))]
        + [_resident(w.shape) for w in weights],
        out_specs=[row_spec, pl.BlockSpec((None, CHUNK, D_GMLP), lambda b, i: (b, 0, 0))],
        out_shape=[jax.ShapeDtypeStruct(x1.shape, F32), jax.ShapeDtypeStruct((bsz, CHUNK, D_GMLP), F32)],
        scratch_shapes=[pltpu.VMEM((tm, D_MODEL), BF16), pltpu.VMEM((tm, D_MODEL), F32),
                        pltpu.VMEM((N_GROUPS, 2 * GROUP_W // LANES, tm, LANES), F32),
                        pltpu.VMEM((tm, D_GMLP), F32)],
        compiler_params=pltpu.CompilerParams(dimension_semantics=("arbitrary", "arbitrary"),
                                             vmem_limit_bytes=VMEM_LIMIT_BYTES),
        name="k_mix_ffn",
    )(x1, *obs, *lses, mod, *weights)


def _mix_ffn_sample_kernel(x1_ref, attnt_ref, mod_ref, ng_ref, wr_ref, lng_ref, lnb_ref,
                           ws0_ref, bs0_ref, wba_ref, wbb_ref, wo_ref, wu_ref, wd_ref, fg_ref,
                           y_ref, vrows_ref, hb_scr, acc_scr):
    x1 = x1_ref[...]
    h1 = _pre(x1, ng_ref[1:2, :], mod_ref, 1).astype(BF16)
    uv = jnp.dot(h1, wr_ref[:, :2 * D_GMLP], preferred_element_type=F32)
    gg = jnp.dot(h1, wr_ref[:, 2 * D_GMLP:], preferred_element_type=F32)
    u = uv[:, :D_GMLP]
    vn = _layer_norm_rows(uv[:, D_GMLP:], lng_ref, lnb_ref)
    vrows_ref[...] = vn
    mixed = ws0_ref[...] * vn + bs0_ref[...]
    attn = attnt_ref[...].T.astype(BF16)
    y_ref[...] = _mix_tail(x1, gg, u, mixed, attn, mod_ref, ng_ref, wba_ref, wbb_ref, wo_ref,
                           wu_ref, wd_ref, fg_ref, hb_scr, acc_scr)


def _mix_ffn_sample(x1, attnt, mod, norm_g, wr, lng, lnb, ws0, bs0, wba, wbb, wo, wu, wd, fg):
    n = x1.shape[0]
    operands = (x1, attnt, mod, norm_g, wr, lng, lnb, ws0, bs0, wba, wbb, wo, wu, wd, fg)
    return pl.pallas_call(
        _mix_ffn_sample_kernel,
        grid=(1,),
        in_specs=[_resident(w.shape) for w in operands],
        out_specs=[_whole(x1.shape), _whole((n, D_GMLP))],
        out_shape=[jax.ShapeDtypeStruct(x1.shape, F32), jax.ShapeDtypeStruct((n, D_GMLP), F32)],
        scratch_shapes=[pltpu.VMEM((n, D_MODEL), BF16), pltpu.VMEM((n, D_MODEL), F32)],
        compiler_params=pltpu.CompilerParams(dimension_semantics=("arbitrary",),
                                             vmem_limit_bytes=VMEM_LIMIT_BYTES),
        name="k_mix_ffn_sample",
    )(*operands)


def _to_buffer_layout(cache):
    n, rows = cache.shape[0], cache.shape[1]
    return cache.transpose(0, 2, 3, 4, 1).reshape(n, 2 * HEADS_PER_GROUP, HEAD_DIM, rows)


def _from_buffer_layout(buf):
    n, rows = buf.shape[0], buf.shape[-1]
    return buf.reshape(n, 2, HEADS_PER_GROUP, HEAD_DIM, rows).transpose(0, 4, 1, 2, 3)[None]


def kernel(x_prompt, x_sample, c_prompt, c_sample, cache_kv_g0, cache_kv_g1, cache_kv_g2, ada_w, ada_b, norm_g,
           ffn1_up, ffn1_down, w_in, w_branch_a, w_branch_b, w_out, v_ln_g, v_ln_b, w_spatial, b_spatial,
           ffn2_up, ffn2_down, final_g):
    assert ada_w.shape[0] == 1, "one layer"
    bsz, seq_len, _ = x_prompt.shape
    n_req = x_sample.shape[0]

    wu1, wd1 = ffn1_up[0].astype(BF16), ffn1_down[0].astype(BF16)
    wu2, wd2 = ffn2_up[0].astype(BF16), ffn2_down[0].astype(BF16)
    wqkv = w_in[0][:, :3 * D_ATTN].astype(BF16)
    wr = w_in[0][:, 3 * D_ATTN:].astype(BF16)
    wkvt = wqkv[:, D_ATTN:].T
    wba, wbb, wo = w_branch_a[0].astype(BF16), w_branch_b[0].astype(BF16), w_out[0].astype(BF16)
    ng = norm_g[0]
    lng, lnb = v_ln_g, v_ln_b
    ws = w_spatial[0]
    bst = b_spatial[0].T
    gw = D_GMLP // GMLP_GROUPS
    ws0 = jnp.repeat(ws[:, 0, 0], gw)[None, :]
    bs0 = jnp.repeat(b_spatial[0][:, 0], gw)[None, :]
    fg = final_g[None, :]

    mods = _ada_table(jnp.concatenate([c_prompt, c_sample], axis=0), ada_w[0], ada_b)
    mod_p = mods[:, :bsz].transpose(1, 0, 2)
    mod_s = mods[:, bsz:]

    x1s, qkvt = _ffn_qkv_sample(x_sample.reshape(n_req, D_MODEL), mod_s, ng, wu1, wd1, wqkv)
    caches = [_to_buffer_layout(c[0]) for c in (cache_kv_g0, cache_kv_g1, cache_kv_g2)]
    (x1, a0, a1, a2, kv0, kv1, kv2, nk0, nk1, nk2, pn0, pn1, pn2, stats) = _ffn_qkv_prompt(
        x_prompt, mod_p, ng, wu1, wd1, wqkv, wkvt, qkvt, caches)

    nq = KEYS_PER_QUERY_BLOCK
    a_blocks = [a.reshape(-1, nq, 3 * GROUP_W) for a in (a0, a1, a2)]
    blocks_per_seq = tuple(seq_len // dil // nq for _, dil in DIL_GROUPS)
    ob0, ob1, ob2, ls0, ls1, ls2, s0, s1, s2, attnt = _attention(
        qkvt, a_blocks, caches, (nk0, nk1, nk2), (pn0, pn1, pn2), stats, blocks_per_seq)
    per_residue = lambda og, dil: og.reshape(bsz, dil, seq_len // dil, GROUP_W)
    obs = [per_residue(og, dil) for og, (_, dil) in zip((ob0, ob1, ob2), DIL_GROUPS)]
    lses = [per_residue(og, dil) for og, (_, dil) in zip((ls0, ls1, ls2), DIL_GROUPS)]

    y_prompt, vrows_p = _mix_ffn_prompt(x1, obs, lses, mod_p, ng, wr, lng, lnb, ws, bst, wba, wbb, wo,
                                        wu2, wd2, fg)
    y_sample, vrows_s = _mix_ffn_sample(x1s, attnt, mod_s, ng, wr, lng, lnb, ws0, bs0, wba, wbb, wo,
                                        wu2, wd2, fg)

    return (y_prompt, y_sample.reshape(n_req, 1, D_MODEL),
            _from_buffer_layout(kv0), _from_buffer_layout(kv1), _from_buffer_layout(kv2), vrows_p[None],
            _from_buffer_layout(s0), _from_buffer_layout(s1), _from_buffer_layout(s2),
            vrows_s.reshape(1, n_req, 1, D_GMLP))
```
